```python
import math
import jax, jax.numpy as jnp
from jax import lax
import numpy as np

D_MODEL = 4096
BATCH = 2
SEQ = 8192
DEPTH = 4

N_META = 16
ATTN_WIDTH = D_MODEL // 2
SSM_WIDTH = D_MODEL - ATTN_WIDTH
MIX_WIDTH = ATTN_WIDTH + SSM_WIDTH
HEAD_DIM = 128
N_ATTN_HEADS = ATTN_WIDTH // HEAD_DIM
Q_BLOCK = 128
SSM_GROUP_CH = 16
SSM_GROUPS = SSM_WIDTH // SSM_GROUP_CH
SSM_STATE = 64
STEP_MIN = 1e-3
STEP_MAX = 1e-1
IN_PROJ_WIDTH = 3 * ATTN_WIDTH + SSM_WIDTH
N_EXPERT_GROUPS = 4
EXPERTS_PER_GROUP = 8
N_EXPERTS = N_EXPERT_GROUPS * EXPERTS_PER_GROUP
EXPERT_TOP_K = 2
D_EXPERT = 3 * D_MODEL // 32
EXPERT_BLOCK = 128
NORM_EPS = 1e-6

kernel_name = 'hymba_s5_stickbreak_hmoe'


def rms_norm(x, gain):
    xf = x.astype(jnp.float32)
    y = xf * lax.rsqrt(jnp.mean(xf * xf, axis=-1, keepdims=True) + NORM_EPS)
    return (y * gain.astype(jnp.float32)).astype(x.dtype)


def stick_breaking_attention(q, k, v):
    length = q.shape[1]
    f32 = jnp.float32
    qf = q.astype(f32) * (HEAD_DIM ** -0.5)
    kf = k.astype(f32)
    vf = v.astype(f32)
    bounds = [(0, N_META)] + [(t0, t0 + Q_BLOCK) for t0 in range(N_META, length, Q_BLOCK)]
    outs = []
    for t0, t1 in bounds:
        z = jnp.einsum('bqhd,bkhd->bhqk', qf[:, t0:t1], kf[:, :t1])
        visible = jnp.arange(t1)[None, :] < jnp.arange(t0, t1)[:, None]
        log_stay = jnp.where(visible, jax.nn.log_sigmoid(-z), 0.0)
        log_pass = lax.cumsum(log_stay, axis=3, reverse=True) - log_stay
        w = jnp.where(visible, jnp.exp(jax.nn.log_sigmoid(z) + log_pass), 0.0)
        outs.append(jnp.einsum('bhqk,bkhd->bqhd', w, vf[:, :t1]))
    return jnp.concatenate(outs, axis=1).astype(q.dtype)


def _complex_linear_combine(earlier, later):
    a1r, a1i, b1r, b1i = earlier
    a2r, a2i, b2r, b2i = later
    return (a2r * a1r - a2i * a1i,
            a2r * a1i + a2i * a1r,
            a2r * b1r - a2i * b1i + b2r,
            a2r * b1i + a2i * b1r + b2i)


def s5_head_group(u, lam_re, lam_im, log_step, b_re, b_im, c_re, c_im, d_skip, w_glu, b_glu):
    bsz, length, _ = u.shape
    f32 = jnp.float32
    uf = u.astype(f32).reshape(bsz, length, SSM_GROUPS, SSM_GROUP_CH)
    lr = lam_re.astype(f32)
    li = lam_im.astype(f32)
    step = jnp.exp(log_step.astype(f32))[:, None]
    mag = jnp.exp(lr * step)
    ang = li * step
    abar_re = mag * jnp.cos(ang)
    abar_im = mag * jnp.sin(ang)
    den = lr * lr + li * li
    num_re = abar_re - 1.0
    coef_re = (num_re * lr + abar_im * li) / den
    coef_im = (abar_im * lr - num_re * li) / den
    br = b_re.astype(f32)
    bi = b_im.astype(f32)
    bbar_re = coef_re[..., None] * br - coef_im[..., None] * bi
    bbar_im = coef_re[..., None] * bi + coef_im[..., None] * br
    bu_re = jnp.einsum('blgc,gpc->blgp', uf, bbar_re)
    bu_im = jnp.einsum('blgc,gpc->blgp', uf, bbar_im)
    a_re = jnp.broadcast_to(abar_re, bu_re.shape)
    a_im = jnp.broadcast_to(abar_im, bu_im.shape)
    _, _, h_re, h_im = lax.associative_scan(_complex_linear_combine, (a_re, a_im, bu_re, bu_im), axis=1)
    y = (jnp.einsum('blgp,gcp->blgc', h_re, c_re.astype(f32))
         - jnp.einsum('blgp,gcp->blgc', h_im, c_im.astype(f32))
         + d_skip.astype(f32).reshape(SSM_GROUPS, SSM_GROUP_CH) * uf)
    g = jax.nn.gelu(y.reshape(bsz, length, SSM_WIDTH).astype(u.dtype))
    return g * jax.nn.sigmoid(g @ w_glu + b_glu)


def hierarchical_moe(h, w_rg, b_rg, w_re, b_re, w_gate, w_up, w_down):
    n_tok, d = h.shape
    f32 = jnp.float32
    p_group = jax.nn.softmax((h @ w_rg).astype(f32) + b_rg.astype(f32), axis=-1)
    g_prob, g_idx = lax.top_k(p_group, 1)
    e_logits = ((h @ w_re).astype(f32) + b_re.astype(f32)).reshape(n_tok, N_EXPERT_GROUPS, EXPERTS_PER_GROUP)
    e_logits = jnp.take_along_axis(e_logits, g_idx[:, :, None], axis=1)[:, 0]
    e_prob, e_idx = lax.top_k(jax.nn.softmax(e_logits, axis=-1), EXPERT_TOP_K)
    e_prob = e_prob / jnp.sum(e_prob, axis=-1, keepdims=True)
    gate = (g_prob * e_prob).reshape(-1)
    expert = (g_idx * EXPERTS_PER_GROUP + e_idx).reshape(-1).astype(jnp.int32)
    token = jnp.repeat(jnp.arange(n_tok, dtype=jnp.int32), EXPERT_TOP_K)
    n_assign = n_tok * EXPERT_TOP_K
    n_blocks = -(-n_assign // EXPERT_BLOCK) + N_EXPERTS
    n_rows = n_blocks * EXPERT_BLOCK
    order = jnp.argsort(expert)
    expert_sorted = expert[order]
    counts = jnp.bincount(expert, length=N_EXPERTS)
    padded = (counts + EXPERT_BLOCK - 1) // EXPERT_BLOCK * EXPERT_BLOCK
    pad_end = jnp.cumsum(padded)
    pad_start = pad_end - padded
    start = jnp.cumsum(counts) - counts
    dest = pad_start[expert_sorted] + jnp.arange(n_assign) - start[expert_sorted]
    row_tok = jnp.zeros((n_rows,), jnp.int32).at[dest].set(token[order])
    row_gate = jnp.zeros((n_rows,), f32).at[dest].set(gate[order])
    block_expert = jnp.minimum(
        jnp.searchsorted(pad_end, jnp.arange(n_blocks) * EXPERT_BLOCK, side='right'), N_EXPERTS - 1)

    def expert_block(args):
        tok, e = args
        xb = h[tok]
        hid = jax.nn.silu(xb @ w_gate[e]) * (xb @ w_up[e])
        return hid @ w_down[e]

    y_rows = lax.map(expert_block, (row_tok.reshape(n_blocks, EXPERT_BLOCK), block_expert))
    y = jax.ops.segment_sum(y_rows.reshape(n_rows, d).astype(f32) * row_gate[:, None],
                            row_tok, num_segments=n_tok)
    return y.astype(h.dtype)


def setup_inputs(seed: int = 0) -> dict:
    key = jax.random.key(seed)
    ks = jax.random.split(key, 26)
    f32 = jnp.float32

    def nrm(k, shape, scale):
        return jax.random.normal(k, shape, f32) * scale

    def gain(k, shape):
        return 1.0 + 0.02 * jax.random.normal(k, shape, f32)

    lam_im_init = math.pi * jnp.arange(SSM_STATE, dtype=f32)
    return {
        'x': nrm(ks[0], (BATCH, SEQ, D_MODEL), 1.0),
        'meta_tokens': nrm(ks[1], (N_META, D_MODEL), 1.0),
        'ln_mix': gain(ks[2], (DEPTH, D_MODEL)),
        'w_in': nrm(ks[3], (DEPTH, D_MODEL, IN_PROJ_WIDTH), D_MODEL ** -0.5),
        'ssm_lam_re': -0.5 + 0.01 * jax.random.normal(ks[4], (DEPTH, SSM_GROUPS, SSM_STATE), f32),
        'ssm_lam_im': lam_im_init + 0.01 * jax.random.normal(ks[5], (DEPTH, SSM_GROUPS, SSM_STATE), f32),
        'ssm_log_step': jax.random.uniform(ks[6], (DEPTH, SSM_GROUPS), f32,
                                           math.log(STEP_MIN), math.log(STEP_MAX)),
        'ssm_b_re': nrm(ks[7], (DEPTH, SSM_GROUPS, SSM_STATE, SSM_GROUP_CH), (2 * SSM_GROUP_CH) ** -0.5),
        'ssm_b_im': nrm(ks[8], (DEPTH, SSM_GROUPS, SSM_STATE, SSM_GROUP_CH), (2 * SSM_GROUP_CH) ** -0.5),
        'ssm_c_re': nrm(ks[9], (DEPTH, SSM_GROUPS, SSM_GROUP_CH, SSM_STATE), SSM_STATE ** -0.5),
        'ssm_c_im': nrm(ks[10], (DEPTH, SSM_GROUPS, SSM_GROUP_CH, SSM_STATE), SSM_STATE ** -0.5),
        'ssm_d': nrm(ks[11], (DEPTH, SSM_WIDTH), 1.0),
        'w_glu': nrm(ks[12], (DEPTH, SSM_WIDTH, SSM_WIDTH), SSM_WIDTH ** -0.5),
        'b_glu': nrm(ks[13], (DEPTH, SSM_WIDTH), 0.01),
        'g_attn_out': gain(ks[14], (DEPTH, ATTN_WIDTH)),
        'g_ssm_out': gain(ks[15], (DEPTH, SSM_WIDTH)),
        'w_out': nrm(ks[16], (DEPTH, MIX_WIDTH, D_MODEL), MIX_WIDTH ** -0.5),
        'ln_ffn': gain(ks[17], (DEPTH, D_MODEL)),
        'w_route_group': nrm(ks[18], (DEPTH, D_MODEL, N_EXPERT_GROUPS), D_MODEL ** -0.5),
        'b_route_group': nrm(ks[19], (DEPTH, N_EXPERT_GROUPS), 0.01),
        'w_route_expert': nrm(ks[20], (DEPTH, D_MODEL, N_EXPERTS), D_MODEL ** -0.5),
        'b_route_expert': nrm(ks[21], (DEPTH, N_EXPERTS), 0.01),
        'w_gate': nrm(ks[22], (DEPTH, N_EXPERTS, D_MODEL, D_EXPERT), D_MODEL ** -0.5),
        'w_up': nrm(ks[23], (DEPTH, N_EXPERTS, D_MODEL, D_EXPERT), D_MODEL ** -0.5),
        'w_down': nrm(ks[24], (DEPTH, N_EXPERTS, D_EXPERT, D_MODEL), D_EXPERT ** -0.5),
        'ln_final': gain(ks[25], (D_MODEL,)),
    }


def reference(x, meta_tokens, ln_mix, w_in, ssm_lam_re, ssm_lam_im, ssm_log_step, ssm_b_re, ssm_b_im,
              ssm_c_re, ssm_c_im, ssm_d, w_glu, b_glu, g_attn_out, g_ssm_out, w_out, ln_ffn,
              w_route_group, b_route_group, w_route_expert, b_route_expert, w_gate, w_up, w_down,
              ln_final):
    bsz = x.shape[0]
    meta = jnp.broadcast_to(meta_tokens.astype(x.dtype)[None], (bsz, N_META, D_MODEL))
    h = jnp.concatenate([meta, x], axis=1)
    length = h.shape[1]
    heads = (bsz, length, N_ATTN_HEADS, HEAD_DIM)
    for l in range(DEPTH):
        hn = rms_norm(h, ln_mix[l])
        proj = hn @ w_in[l]
        q, k, v, u = jnp.split(proj, [ATTN_WIDTH, 2 * ATTN_WIDTH, 3 * ATTN_WIDTH], axis=-1)
        attn = stick_breaking_attention(q.reshape(heads), k.reshape(heads), v.reshape(heads))
        attn = attn.reshape(bsz, length, ATTN_WIDTH)
        ssm = s5_head_group(u, ssm_lam_re[l], ssm_lam_im[l], ssm_log_step[l], ssm_b_re[l], ssm_b_im[l],
                            ssm_c_re[l], ssm_c_im[l], ssm_d[l], w_glu[l], b_glu[l])
        mixed = jnp.concatenate([rms_norm(attn, g_attn_out[l]), rms_norm(ssm, g_ssm_out[l])], axis=-1)
        h = h + mixed @ w_out[l]
        hn = rms_norm(h, ln_ffn[l])
        moe_out = hierarchical_moe(hn.reshape(bsz * length, D_MODEL), w_route_group[l], b_route_group[l],
                                   w_route_expert[l], b_route_expert[l], w_gate[l], w_up[l], w_down[l])
        h = h + moe_out.reshape(bsz, length, D_MODEL)
    return rms_norm(h[:, N_META:], ln_final)
```

```python
import functools
import math

import jax
import jax.numpy as jnp
from jax import lax
from jax.experimental import pallas as pl
from jax.experimental.pallas import tpu as pltpu

F32 = jnp.float32
BF16 = jnp.bfloat16

LANES = 128
HEAD_DIM = 128
Q_TILE = 128
ROW_ALIGN = 128
ATTN_WINDOW = 3
SSM_CHUNK = 8
EXPERT_ROWS = 256
COMBINE_ROWS = 128
NORM_EPS = 1e-6
EXP_UNDERFLOW = -104.0
MASKED = -1e30
VMEM_LIMIT = 56 * 1024 * 1024


def _pick_tile(n, target, align):
    best = align
    for t in range(align, min(n, target) + 1, align):
        if n % t == 0:
            best = t
    return best


def _params(sem, vmem=VMEM_LIMIT):
    return pltpu.CompilerParams(dimension_semantics=sem, vmem_limit_bytes=vmem)


def _rms(x, gain):
    return x * lax.rsqrt(jnp.mean(x * x, axis=-1, keepdims=True) + NORM_EPS) * gain


def _norm_kernel(x_ref, g_ref, o_ref):
    o_ref[...] = _rms(x_ref[...], g_ref[...]).astype(o_ref.dtype)


def rmsnorm(x, gain, out_dtype):
    t, d = x.shape
    tm = _pick_tile(t, 512, ROW_ALIGN)
    return pl.pallas_call(
        _norm_kernel,
        grid=(t // tm,),
        in_specs=[pl.BlockSpec((tm, d), lambda i: (i, 0)),
                  pl.BlockSpec((1, d), lambda i: (0, 0))],
        out_specs=pl.BlockSpec((tm, d), lambda i: (i, 0)),
        out_shape=jax.ShapeDtypeStruct((t, d), out_dtype),
        compiler_params=_params(("parallel",)),
        name="rmsnorm",
    )(x, gain.reshape(1, d))


def _mm_kernel(a_ref, w_ref, o_ref):
    o_ref[...] = jnp.dot(a_ref[...], w_ref[...], preferred_element_type=F32).astype(o_ref.dtype)


def in_projection(a, w_stack, layer, col0, ncols, out_dtype):
    t, k = a.shape
    tm = _pick_tile(t, 1280, ROW_ALIGN)
    tn = _pick_tile(ncols, 512, LANES)
    assert col0 % tn == 0
    return pl.pallas_call(
        _mm_kernel,
        grid=(t // tm, ncols // tn),
        in_specs=[pl.BlockSpec((tm, k), lambda i, j: (i, 0)),
                  pl.BlockSpec((None, k, tn), lambda i, j: (layer, 0, col0 // tn + j))],
        out_specs=pl.BlockSpec((tm, tn), lambda i, j: (i, j)),
        out_shape=jax.ShapeDtypeStruct((t, ncols), out_dtype),
        compiler_params=_params(("parallel", "arbitrary")),
        name="in_projection",
    )(a, w_stack)


def _out_proj_kernel(a1_ref, a2_ref, w1_ref, w2_ref, r_ref, o_ref):
    acc = jnp.dot(a1_ref[...], w1_ref[...], preferred_element_type=F32)
    acc += jnp.dot(a2_ref[...], w2_ref[...], preferred_element_type=F32)
    o_ref[...] = r_ref[...] + acc


def out_projection(a1, a2, w_stack, layer, resid):
    t, kh = a1.shape
    n = w_stack.shape[-1]
    tm = _pick_tile(t, 1280, ROW_ALIGN)
    tn = _pick_tile(n, 512, LANES)
    return pl.pallas_call(
        _out_proj_kernel,
        grid=(t // tm, n // tn),
        in_specs=[pl.BlockSpec((tm, kh), lambda i, j: (i, 0)),
                  pl.BlockSpec((tm, kh), lambda i, j: (i, 0)),
                  pl.BlockSpec((None, kh, tn), lambda i, j: (layer, 0, j)),
                  pl.BlockSpec((None, kh, tn), lambda i, j: (layer, 1, j)),
                  pl.BlockSpec((tm, tn), lambda i, j: (i, j))],
        out_specs=pl.BlockSpec((tm, tn), lambda i, j: (i, j)),
        out_shape=jax.ShapeDtypeStruct((t, n), F32),
        compiler_params=_params(("parallel", "arbitrary")),
        name="out_projection",
    )(a1, a2, w_stack, w_stack, resid)


def _attn_chunk(q_ref, k_ref, v_ref, tri_ref, carry_ref, acc_ref, h, *, diagonal, first):
    hs = pl.ds(pl.multiple_of(h * HEAD_DIM, HEAD_DIM), HEAD_DIM)
    q = q_ref[:, hs]
    k = k_ref[:, hs]
    v = v_ref[:, hs]
    z = lax.dot_general(q, k, (((1,), (1,)), ((), ())), preferred_element_type=F32)
    z = z * (HEAD_DIM ** -0.5)
    log_stay = -(jnp.maximum(z, 0.0) + jnp.log(1.0 + jnp.exp(-jnp.abs(z))))
    if diagonal:
        row = lax.broadcasted_iota(jnp.int32, z.shape, 0)
        col = lax.broadcasted_iota(jnp.int32, z.shape, 1)
        visible = col < row
        log_stay = jnp.where(visible, log_stay, 0.0)
    hi = log_stay.astype(BF16)
    lo = (log_stay - hi.astype(F32)).astype(BF16)
    sums = jnp.dot(jnp.concatenate([hi, lo], axis=1), tri_ref[...], preferred_element_type=F32)
    expo = z + sums[:, :Q_TILE]
    if not first:
        carry = carry_ref[h]
        expo = expo + carry
    if diagonal:
        expo = jnp.where(visible, expo, MASKED)
    w = jnp.exp(expo).astype(BF16)
    pv = jnp.dot(w, v, preferred_element_type=F32)
    if first:
        acc_ref[:, hs] = pv
        carry_ref[h] = sums[:, Q_TILE:]
    else:
        acc_ref[:, hs] += pv
        carry_ref[h] = carry + sums[:, Q_TILE:]


def _attn_kernel(q_ref, k0_ref, k1_ref, k2_ref, v0_ref, v1_ref, v2_ref, kv_hbm, tri_ref, g_ref,
                 o_ref, carry_ref, acc_ref, kt_ref, vt_ref, sem, *, n_heads, n_qblocks):
    b = pl.program_id(0)
    i = pl.program_id(1)
    width = n_heads * HEAD_DIM
    k_refs = (k0_ref, k1_ref, k2_ref)
    v_refs = (v0_ref, v1_ref, v2_ref)

    def sweep(k_ref, v_ref, diagonal, first):
        def body(h, c):
            _attn_chunk(q_ref, k_ref, v_ref, tri_ref, carry_ref, acc_ref, h,
                        diagonal=diagonal, first=first)
            return c
        lax.fori_loop(0, n_heads, body, 0)

    sweep(k_refs[0], v_refs[0], True, True)
    for c in range(1, ATTN_WINDOW):
        @pl.when(i >= c)
        def _(c=c):
            sweep(k_refs[c], v_refs[c], False, False)

    def weight_left():
        m = carry_ref[0]
        for h in range(1, n_heads):
            m = jnp.maximum(m, carry_ref[h])
        return jnp.max(m) > EXP_UNDERFLOW

    def tail_cond(state):
        kb, more = state
        return jnp.logical_and(kb >= 0, more)

    def tail_body(state):
        kb, _ = state
        row0 = pl.multiple_of((b * n_qblocks + kb) * Q_TILE, Q_TILE)
        ck = pltpu.make_async_copy(kv_hbm.at[pl.ds(row0, Q_TILE), pl.ds(width, width)], kt_ref, sem.at[0])
        cv = pltpu.make_async_copy(kv_hbm.at[pl.ds(row0, Q_TILE), pl.ds(2 * width, width)], vt_ref, sem.at[1])
        ck.start()
        cv.start()
        ck.wait()
        cv.wait()
        sweep(kt_ref, vt_ref, False, False)
        return kb - 1, weight_left()

    lax.while_loop(tail_cond, tail_body, (i - ATTN_WINDOW, weight_left()))
    o_ref[...] = _rms(acc_ref[...], g_ref[...]).astype(o_ref.dtype)


def attention(qkv, gain, batch, n_heads):
    t = qkv.shape[0]
    width = n_heads * HEAD_DIM
    nq = t // batch // Q_TILE
    j = jnp.arange(2 * Q_TILE)[:, None] % Q_TILE
    s = jnp.arange(2 * Q_TILE)[None, :]
    tri = jnp.where(s < Q_TILE, j >= s, True).astype(BF16)

    def qmap(b, i):
        return (b * nq + i, 0)

    def kvmap(c, col):
        return lambda b, i: (b * nq + jnp.maximum(i - c, 0), col)

    blk = (Q_TILE, width)
    return pl.pallas_call(
        functools.partial(_attn_kernel, n_heads=n_heads, n_qblocks=nq),
        grid=(batch, nq),
        in_specs=[pl.BlockSpec(blk, qmap)]
        + [pl.BlockSpec(blk, kvmap(c, 1)) for c in range(ATTN_WINDOW)]
        + [pl.BlockSpec(blk, kvmap(c, 2)) for c in range(ATTN_WINDOW)]
        + [pl.BlockSpec(memory_space=pl.ANY),
           pl.BlockSpec((2 * Q_TILE, 2 * Q_TILE), lambda b, i: (0, 0)),
           pl.BlockSpec((1, width), lambda b, i: (0, 0))],
        out_specs=pl.BlockSpec(blk, qmap),
        out_shape=jax.ShapeDtypeStruct((t, width), BF16),
        scratch_shapes=[pltpu.VMEM((n_heads, Q_TILE, Q_TILE), F32),
                        pltpu.VMEM((Q_TILE, width), F32),
                        pltpu.VMEM((Q_TILE, width), BF16),
                        pltpu.VMEM((Q_TILE, width), BF16),
                        pltpu.SemaphoreType.DMA((2,))],
        compiler_params=_params(("parallel", "arbitrary")),
        name="stickbreak_attention",
    )(qkv, qkv, qkv, qkv, qkv, qkv, qkv, qkv, tri, gain.reshape(1, width))


def _ssm_operators(lam_re, lam_im, log_step, b_re, b_im, c_re, c_im):
    hp = lax.Precision.HIGHEST
    g, p, c = b_re.shape
    gp = LANES // c
    jt = g // gp
    tc = SSM_CHUNK
    lr = lam_re.astype(F32)
    li = lam_im.astype(F32)
    step = jnp.exp(log_step.astype(F32))[:, None]
    tau = jnp.arange(tc + 1, dtype=F32)[:, None, None]
    mag = jnp.exp(lr * step * tau)
    ang = li * step * tau
    pw_re = mag * jnp.cos(ang)
    pw_im = mag * jnp.sin(ang)
    abar_re, abar_im = pw_re[1], pw_im[1]
    den = lr * lr + li * li
    num_re = abar_re - 1.0
    coef_re = (num_re * lr + abar_im * li) / den
    coef_im = (abar_im * lr - num_re * li) / den
    br = b_re.astype(F32)
    bi = b_im.astype(F32)
    bbar_re = coef_re[..., None] * br - coef_im[..., None] * bi
    bbar_im = coef_re[..., None] * bi + coef_im[..., None] * br
    cr = c_re.astype(F32)[None]
    ci = c_im.astype(F32)[None]
    cp_re = cr * pw_re[:, :, None, :] - ci * pw_im[:, :, None, :]
    cp_im = cr * pw_im[:, :, None, :] + ci * pw_re[:, :, None, :]
    kern = (jnp.einsum('tgop,gpi->tgoi', cp_re[:tc], bbar_re, precision=hp)
            - jnp.einsum('tgop,gpi->tgoi', cp_im[:tc], bbar_im, precision=hp))
    eye = jnp.eye(gp, dtype=F32)
    s_idx = jnp.arange(tc)[:, None]
    r_idx = jnp.arange(tc)[None, :]
    lag = r_idx - s_idx
    kt = kern[jnp.clip(lag, 0, tc - 1)] * (lag >= 0)[:, :, None, None, None].astype(F32)
    kt = kt.reshape(tc, tc, jt, gp, c, c)
    m_intra = jnp.einsum('srjaoi,ab->jsairbo', kt, eye).reshape(jt, tc * LANES, tc * LANES)
    rev = tc - 1 - jnp.arange(tc)
    win_re = (pw_re[rev][..., None] * bbar_re[None] - pw_im[rev][..., None] * bbar_im[None])
    win_im = (pw_re[rev][..., None] * bbar_im[None] + pw_im[rev][..., None] * bbar_re[None])
    def in_block(wx):
        wx = wx.reshape(tc, jt, gp, p, c)
        return jnp.einsum('sjapi,ab->jsaibp', wx, eye).reshape(jt, tc * LANES, gp * p)
    w_in = jnp.concatenate([in_block(win_re), in_block(win_im)], axis=-1)
    def out_block(cx):
        cx = cx[1:].reshape(tc, jt, gp, c, p)
        return jnp.einsum('rjaop,ab->japrbo', cx, eye).reshape(jt, gp * p, tc * LANES)
    w_out = jnp.concatenate([out_block(cp_re), -out_block(cp_im)], axis=1)
    a_chunk = jnp.concatenate([pw_re[tc].reshape(jt, 1, gp * p), pw_im[tc].reshape(jt, 1, gp * p)], axis=-1)
    return m_intra.astype(BF16), w_in.astype(BF16), w_out.astype(BF16), a_chunk


def _gelu_tanh(x):
    return 0.5 * x * (1.0 + jnp.tanh(math.sqrt(2.0 / math.pi) * (x + 0.044715 * (x * x * x))))


def _ssm_kernel(u_ref, m_ref, win_ref, wout_ref, a_ref, d_ref, o_ref, e_ref, sprev_ref, *, n_chunks):
    tc = SSM_CHUNK
    half = a_ref.shape[-1] // 2
    uv = jnp.concatenate([u_ref[pl.ds(r, n_chunks, stride=tc), :] for r in range(tc)], axis=1)
    uvb = uv.astype(BF16)
    e_ref[...] = jnp.dot(uvb, win_ref[...], preferred_element_type=F32)
    a_re = a_ref[:, :half]
    a_im = a_ref[:, half:]

    def step(n, state):
        s_re, s_im = state
        sprev_ref[pl.ds(n, 1), :half] = s_re
        sprev_ref[pl.ds(n, 1), half:] = s_im
        e = e_ref[pl.ds(n, 1), :]
        return (a_re * s_re - a_im * s_im + e[:, :half],
                a_re * s_im + a_im * s_re + e[:, half:])

    zero = jnp.zeros((1, half), F32)
    lax.fori_loop(0, n_chunks, step, (zero, zero))
    y = jnp.dot(uvb, m_ref[...], preferred_element_type=F32)
    y += jnp.dot(sprev_ref[...].astype(BF16), wout_ref[...], preferred_element_type=F32)
    d = d_ref[...]
    for r in range(tc):
        cols = slice(r * LANES, (r + 1) * LANES)
        o_ref[pl.ds(r, n_chunks, stride=tc), :] = _gelu_tanh(y[:, cols] + d * uv[:, cols])


def ssm_scan(u, ops, d_skip, batch):
    m_intra, w_in, w_out, a_chunk = ops
    t, width = u.shape
    lp = t // batch
    jt = width // LANES
    nc = lp // SSM_CHUNK
    kdim = SSM_CHUNK * LANES
    sdim = a_chunk.shape[-1]
    return pl.pallas_call(
        functools.partial(_ssm_kernel, n_chunks=nc),
        grid=(jt, batch),
        in_specs=[pl.BlockSpec((lp, LANES), lambda j, b: (b, j)),
                  pl.BlockSpec((None, kdim, kdim), lambda j, b: (j, 0, 0)),
                  pl.BlockSpec((None, kdim, sdim), lambda j, b: (j, 0, 0)),
                  pl.BlockSpec((None, sdim, kdim), lambda j, b: (j, 0, 0)),
                  pl.BlockSpec((None, 1, sdim), lambda j, b: (j, 0, 0)),
                  pl.BlockSpec((1, LANES), lambda j, b: (0, j))],
        out_specs=pl.BlockSpec((lp, LANES), lambda j, b: (b, j)),
        out_shape=jax.ShapeDtypeStruct((t, width), F32),
        scratch_shapes=[pltpu.VMEM((nc, sdim), F32), pltpu.VMEM((nc, sdim), F32)],
        compiler_params=_params(("parallel", "arbitrary")),
        name="ssm_scan",
    )(u, m_intra, w_in, w_out, a_chunk, d_skip.reshape(1, width))


def _glu_kernel(g_ref, w_ref, b_ref, gain_ref, o_ref):
    g = g_ref[...]
    zz = jnp.dot(g.astype(BF16), w_ref[...], preferred_element_type=F32) + b_ref[...]
    y = g * (1.0 / (1.0 + jnp.exp(-zz)))
    o_ref[...] = _rms(y, gain_ref[...]).astype(o_ref.dtype)


def glu_norm(g, w_stack, layer, bias, gain):
    t, width = g.shape
    tm = _pick_tile(t, 640, ROW_ALIGN)
    return pl.pallas_call(
        _glu_kernel,
        grid=(t // tm,),
        in_specs=[pl.BlockSpec((tm, width), lambda i: (i, 0)),
                  pl.BlockSpec((None, width, width), lambda i: (layer, 0, 0)),
                  pl.BlockSpec((1, width), lambda i: (0, 0)),
                  pl.BlockSpec((1, width), lambda i: (0, 0))],
        out_specs=pl.BlockSpec((tm, width), lambda i: (i, 0)),
        out_shape=jax.ShapeDtypeStruct((t, width), BF16),
        compiler_params=_params(("parallel",)),
        name="glu_norm",
    )(g, w_stack, bias.reshape(1, width), gain.reshape(1, width))


def _route_kernel(h_ref, gain_ref, wh_ref, wl_ref, bias_ref, hn_ref, eid_ref, gate_ref, *, n_groups, per_group):
    hn = _rms(h_ref[...], gain_ref[...])
    hn_ref[...] = hn
    xh = hn.astype(BF16)
    xl = (hn - xh.astype(F32)).astype(BF16)
    wh = wh_ref[...]
    logits = (jnp.dot(xh, wh, preferred_element_type=F32) + jnp.dot(xl, wh, preferred_element_type=F32)
              + jnp.dot(xh, wl_ref[...], preferred_element_type=F32) + bias_ref[...])
    lane = lax.broadcasted_iota(jnp.int32, logits.shape, 1)

    def first_lane_of_max(vals, top):
        return jnp.min(jnp.where(vals == top, lane, LANES), axis=-1, keepdims=True)

    gmask = lane < n_groups
    lg = jnp.where(gmask, logits, MASKED)
    g_top = jnp.max(lg, axis=-1, keepdims=True)
    g_prob = 1.0 / jnp.sum(jnp.where(gmask, jnp.exp(lg - g_top), 0.0), axis=-1, keepdims=True)
    g_idx = first_lane_of_max(lg, g_top)
    lo_lane = n_groups + g_idx * per_group
    emask = jnp.logical_and(lane >= lo_lane, lane < lo_lane + per_group)
    le = jnp.where(emask, logits, MASKED)
    m1 = jnp.max(le, axis=-1, keepdims=True)
    i1 = first_lane_of_max(le, m1)
    denom = jnp.sum(jnp.where(emask, jnp.exp(le - m1), 0.0), axis=-1, keepdims=True)
    le2 = jnp.where(lane == i1, MASKED, le)
    m2 = jnp.max(le2, axis=-1, keepdims=True)
    i2 = first_lane_of_max(le2, m2)
    p1 = 1.0 / denom
    p2 = jnp.exp(m2 - m1) / denom
    psum = p1 + p2
    eid_ref[...] = jnp.where(lane == 0, i1 - n_groups, jnp.where(lane == 1, i2 - n_groups, 0))
    gate_ref[...] = jnp.where(lane == 0, g_prob * (p1 / psum), jnp.where(lane == 1, g_prob * (p2 / psum), 0.0))


def norm_route(h, gain, w_router, b_router, n_groups, per_group):
    t, d = h.shape
    tm = _pick_tile(t, 256, ROW_ALIGN)
    n_out = w_router.shape[-1]
    w_pad = jnp.zeros((d, LANES), F32).at[:, :n_out].set(w_router)
    wh = w_pad.astype(BF16)
    wl = (w_pad - wh.astype(F32)).astype(BF16)
    bias = jnp.zeros((1, LANES), F32).at[0, :n_out].set(b_router)
    return pl.pallas_call(
        functools.partial(_route_kernel, n_groups=n_groups, per_group=per_group),
        grid=(t // tm,),
        in_specs=[pl.BlockSpec((tm, d), lambda i: (i, 0)),
                  pl.BlockSpec((1, d), lambda i: (0, 0)),
                  pl.BlockSpec((d, LANES), lambda i: (0, 0)),
                  pl.BlockSpec((d, LANES), lambda i: (0, 0)),
                  pl.BlockSpec((1, LANES), lambda i: (0, 0))],
        out_specs=[pl.BlockSpec((tm, d), lambda i: (i, 0)),
                   pl.BlockSpec((tm, LANES), lambda i: (i, 0)),
                   pl.BlockSpec((tm, LANES), lambda i: (i, 0))],
        out_shape=[jax.ShapeDtypeStruct((t, d), F32),
                   jax.ShapeDtypeStruct((t, LANES), jnp.int32),
                   jax.ShapeDtypeStruct((t, LANES), F32)],
        compiler_params=_params(("parallel",)),
        name="norm_route",
    )(h, gain.reshape(1, d), wh, wl, bias)


def dispatch_plan(eid, gate, n_experts):
    n_tok = eid.shape[0]
    top_k = eid.shape[1]
    n_assign = n_tok * top_k
    n_blocks = -(-n_assign // EXPERT_ROWS) + n_experts
    n_rows = n_blocks * EXPERT_ROWS
    expert = eid.reshape(-1)
    order = jnp.argsort(expert)
    expert_sorted = expert[order]
    counts = jnp.bincount(expert, length=n_experts)
    padded = (counts + EXPERT_ROWS - 1) // EXPERT_ROWS * EXPERT_ROWS
    pad_end = jnp.cumsum(padded)
    pad_start = pad_end - padded
    start = jnp.cumsum(counts) - counts
    dest = (pad_start[expert_sorted] + jnp.arange(n_assign) - start[expert_sorted]).astype(jnp.int32)
    row_tok = jnp.zeros((n_rows,), jnp.int32).at[dest].set((order // top_k).astype(jnp.int32))
    row_gate = jnp.zeros((n_rows,), F32).at[dest].set(gate.reshape(-1)[order])
    pos = jnp.zeros((n_assign,), jnp.int32).at[order].set(dest)
    block_expert = jnp.minimum(
        jnp.searchsorted(pad_end, jnp.arange(n_blocks) * EXPERT_ROWS, side='right'), n_experts - 1)
    n_used = (pad_end[-1] // EXPERT_ROWS).astype(jnp.int32).reshape(1)
    return row_tok, row_gate.reshape(n_rows, 1), pos, block_expert.astype(jnp.int32), n_used


def _gather_rows(src_hbm, dst_ref, sem, idx_ref, base, n_rows):
    def body(r, c):
        row = idx_ref[base + r]
        pltpu.make_async_copy(src_hbm.at[pl.ds(row, 1), :], dst_ref.at[pl.ds(r, 1), :], sem).start()
        return c
    lax.fori_loop(0, n_rows, body, 0)


def _wait_rows(src_hbm, dst_ref, sem, n_rows):
    pltpu.make_async_copy(src_hbm.at[pl.ds(0, n_rows), :], dst_ref, sem).wait()


def _moe_kernel(be_ref, nu_ref, rt_ref, x_hbm, gate_ref, wg_ref, wu_ref, wd_ref, o_ref, xbuf, sem):
    i = pl.program_id(0)
    n_used = nu_ref[0]
    slot = lax.rem(i, 2)

    def fetch(blk, s):
        _gather_rows(x_hbm, xbuf.at[s], sem.at[s], rt_ref, blk * EXPERT_ROWS, EXPERT_ROWS)

    @pl.when(jnp.logical_and(i == 0, n_used > 0))
    def _():
        fetch(0, 0)

    @pl.when(i + 1 < n_used)
    def _():
        fetch(i + 1, 1 - slot)

    @pl.when(i < n_used)
    def _():
        _wait_rows(x_hbm, xbuf.at[slot], sem.at[slot], EXPERT_ROWS)
        x = xbuf[slot].astype(BF16)
        hg = jnp.dot(x, wg_ref[...], preferred_element_type=F32)
        hu = jnp.dot(x, wu_ref[...], preferred_element_type=F32)
        hid = hg * (1.0 / (1.0 + jnp.exp(-hg))) * hu * gate_ref[...]
        o_ref[...] = jnp.dot(hid.astype(BF16), wd_ref[...], preferred_element_type=F32)

    @pl.when(i >= n_used)
    def _():
        o_ref[...] = jnp.zeros_like(o_ref)


def expert_blocks(hn, plan, w_gate, w_up, w_down, layer):
    row_tok, row_gate, _, block_expert, n_used = plan
    d = hn.shape[1]
    f = w_gate.shape[-1]
    n_rows = row_tok.shape[0]
    n_blocks = n_rows // EXPERT_ROWS
    return pl.pallas_call(
        _moe_kernel,
        grid_spec=pltpu.PrefetchScalarGridSpec(
            num_scalar_prefetch=3,
            grid=(n_blocks,),
            in_specs=[pl.BlockSpec(memory_space=pl.ANY),
                      pl.BlockSpec((EXPERT_ROWS, 1), lambda i, be, nu, rt: (i, 0)),
                      pl.BlockSpec((None, None, d, f), lambda i, be, nu, rt: (layer, be[i], 0, 0)),
                      pl.BlockSpec((None, None, d, f), lambda i, be, nu, rt: (layer, be[i], 0, 0)),
                      pl.BlockSpec((None, None, f, d), lambda i, be, nu, rt: (layer, be[i], 0, 0))],
            out_specs=pl.BlockSpec((EXPERT_ROWS, d), lambda i, be, nu, rt: (i, 0)),
            scratch_shapes=[pltpu.VMEM((2, EXPERT_ROWS, d), F32), pltpu.SemaphoreType.DMA((2,))]),
        out_shape=jax.ShapeDtypeStruct((n_rows, d), F32),
        compiler_params=_params(("arbitrary",)),
        name="expert_blocks",
    )(block_expert, n_used, row_tok, hn, row_gate, w_gate, w_up, w_down)


def _combine_kernel(pos_ref, y_hbm, h_ref, gain_ref, oh_ref, on_ref, ybuf, sem, *, top_k):
    i = pl.program_id(0)
    n = pl.num_programs(0)
    slot = lax.rem(i, 2)
    rows = COMBINE_ROWS * top_k

    def fetch(blk, s):
        _gather_rows(y_hbm, ybuf.at[s], sem.at[s], pos_ref, blk * rows, rows)

    @pl.when(i == 0)
    def _():
        fetch(0, 0)

    @pl.when(i + 1 < n)
    def _():
        fetch(i + 1, 1 - slot)

    _wait_rows(y_hbm, ybuf.at[slot], sem.at[slot], rows)
    d = h_ref.shape[-1]
    y = ybuf[slot].reshape(COMBINE_ROWS, top_k, d)
    h_new = h_ref[...] + jnp.sum(y, axis=1)
    oh_ref[...] = h_new
    on_ref[...] = _rms(h_new, gain_ref[...]).astype(on_ref.dtype)


def combine(h, y_rows, pos, gain, norm_dtype, top_k):
    t, d = h.shape
    return pl.pallas_call(
        functools.partial(_combine_kernel, top_k=top_k),
        grid_spec=pltpu.PrefetchScalarGridSpec(
            num_scalar_prefetch=1,
            grid=(t // COMBINE_ROWS,),
            in_specs=[pl.BlockSpec(memory_space=pl.ANY),
                      pl.BlockSpec((COMBINE_ROWS, d), lambda i, pos: (i, 0)),
                      pl.BlockSpec((1, d), lambda i, pos: (0, 0))],
            out_specs=[pl.BlockSpec((COMBINE_ROWS, d), lambda i, pos: (i, 0)),
                       pl.BlockSpec((COMBINE_ROWS, d), lambda i, pos: (i, 0))],
            scratch_shapes=[pltpu.VMEM((2, COMBINE_ROWS * top_k, d), F32), pltpu.SemaphoreType.DMA((2,))]),
        out_shape=[jax.ShapeDtypeStruct((t, d), F32), jax.ShapeDtypeStruct((t, d), norm_dtype)],
        compiler_params=_params(("arbitrary",)),
        name="combine",
    )(pos, y_rows, h, gain.reshape(1, d))


def kernel(x, meta_tokens, ln_mix, w_in, ssm_lam_re, ssm_lam_im, ssm_log_step, ssm_b_re, ssm_b_im, ssm_c_re, ssm_c_im, ssm_d, w_glu, b_glu, g_attn_out, g_ssm_out, w_out, ln_ffn, w_route_group, b_route_group, w_route_expert, b_route_expert, w_gate, w_up, w_down, ln_final):
    batch, seq, d_model = x.shape
    depth = w_in.shape[0]
    n_meta = meta_tokens.shape[0]
    ssm_width = ssm_d.shape[-1]
    attn_width = g_attn_out.shape[-1]
    n_heads = attn_width // HEAD_DIM
    n_groups = w_route_group.shape[-1]
    n_experts = w_route_expert.shape[-1]
    top_k = 2
    length = n_meta + seq
    lp = -(-length // ROW_ALIGN) * ROW_ALIGN
    t = batch * lp

    meta = jnp.broadcast_to(meta_tokens.astype(x.dtype)[None], (batch, n_meta, d_model))
    h = jnp.concatenate([meta, x, jnp.zeros((batch, lp - length, d_model), x.dtype)], axis=1).reshape(t, d_model)

    w_in_b = w_in.astype(BF16)
    w_glu_b = w_glu.astype(BF16)
    w_out_b = w_out.astype(BF16)
    w_gate_b = w_gate.astype(BF16)
    w_up_b = w_up.astype(BF16)
    w_down_b = w_down.astype(BF16)
    assert attn_width == ssm_width

    hn = rmsnorm(h, ln_mix[0], BF16)
    for l in range(depth):
        qkv = in_projection(hn, w_in_b, l, 0, 3 * attn_width, BF16)
        u = in_projection(hn, w_in_b, l, 3 * attn_width, ssm_width, F32)
        attn_n = attention(qkv, g_attn_out[l], batch, n_heads)
        ops = _ssm_operators(ssm_lam_re[l], ssm_lam_im[l], ssm_log_step[l], ssm_b_re[l], ssm_b_im[l],
                             ssm_c_re[l], ssm_c_im[l])
        g = ssm_scan(u, ops, ssm_d[l], batch)
        ssm_n = glu_norm(g, w_glu_b, l, b_glu[l], g_ssm_out[l])
        h = out_projection(attn_n, ssm_n, w_out_b, l, h)
        w_router = jnp.concatenate([w_route_group[l], w_route_expert[l]], axis=-1)
        b_router = jnp.concatenate([b_route_group[l], b_route_expert[l]], axis=-1)
        hn2, eid, gate = norm_route(h, ln_ffn[l], w_router, b_router, n_groups, n_experts // n_groups)
        plan = dispatch_plan(eid[:, :top_k], gate[:, :top_k], n_experts)
        y_rows = expert_blocks(hn2, plan, w_gate_b, w_up_b, w_down_b, l)
        last = l == depth - 1
        next_gain = ln_final if last else ln_mix[l + 1]
        h, hn = combine(h, y_rows, plan[2], next_gain, F32 if last else BF16, top_k)
    return hn.reshape(batch, lp, d_model)[:, n_meta:length]
```

```python
import functools
import math

import jax
import jax.numpy as jnp
from jax import lax
from jax.experimental import pallas as pl
from jax.experimental.pallas import tpu as pltpu

F32 = jnp.float32
BF16 = jnp.bfloat16

LANES = 128
HEAD_DIM = 128
Q_TILE = 128
ROW_ALIGN = 128
ATTN_WINDOW = 3
SSM_CHUNK = 8
EXPERT_ROWS = 256
COMBINE_ROWS = 128
GATHER_UNROLL = 8
NORM_EPS = 1e-6
EXP_UNDERFLOW = -104.0
MASKED = -1e30
VMEM_LIMIT = 56 * 1024 * 1024


def _pick_tile(n, target, align):
    best = align
    for t in range(align, min(n, target) + 1, align):
        if n % t == 0:
            best = t
    return best


def _params(sem, vmem=VMEM_LIMIT):
    return pltpu.CompilerParams(dimension_semantics=sem, vmem_limit_bytes=vmem)


def _rms(x, gain):
    return x * lax.rsqrt(jnp.mean(x * x, axis=-1, keepdims=True) + NORM_EPS) * gain


def _split_bf16(x):
    hi = x.astype(BF16)
    return hi, (x - hi.astype(F32)).astype(BF16)


def _norm_kernel(x_ref, g_ref, o_ref):
    o_ref[...] = _rms(x_ref[...], g_ref[...]).astype(o_ref.dtype)


def rmsnorm(x, gain, out_dtype):
    t, d = x.shape
    tm = _pick_tile(t, 512, ROW_ALIGN)
    return pl.pallas_call(
        _norm_kernel,
        grid=(t // tm,),
        in_specs=[pl.BlockSpec((tm, d), lambda i: (i, 0)),
                  pl.BlockSpec((1, d), lambda i: (0, 0))],
        out_specs=pl.BlockSpec((tm, d), lambda i: (i, 0)),
        out_shape=jax.ShapeDtypeStruct((t, d), out_dtype),
        compiler_params=_params(("parallel",)),
        name="rmsnorm",
    )(x, gain.reshape(1, d))


def _mm_kernel(a_ref, w_ref, o_ref):
    o_ref[...] = jnp.dot(a_ref[...], w_ref[...], preferred_element_type=F32).astype(o_ref.dtype)


def in_projection(a, w_stack, layer, col0, ncols, out_dtype):
    t, k = a.shape
    tm = _pick_tile(t, 1280, ROW_ALIGN)
    tn = _pick_tile(ncols, 512, LANES)
    assert col0 % tn == 0
    return pl.pallas_call(
        _mm_kernel,
        grid=(t // tm, ncols // tn),
        in_specs=[pl.BlockSpec((tm, k), lambda i, j: (i, 0)),
                  pl.BlockSpec((None, k, tn), lambda i, j: (layer, 0, col0 // tn + j))],
        out_specs=pl.BlockSpec((tm, tn), lambda i, j: (i, j)),
        out_shape=jax.ShapeDtypeStruct((t, ncols), out_dtype),
        compiler_params=_params(("parallel", "arbitrary")),
        name="in_projection",
    )(a, w_stack)


def _out_proj_kernel(a1_ref, a2_ref, w1_ref, w2_ref, r_ref, o_ref):
    acc = jnp.dot(a1_ref[...], w1_ref[...], preferred_element_type=F32)
    acc += jnp.dot(a2_ref[...], w2_ref[...], preferred_element_type=F32)
    o_ref[...] = r_ref[...] + acc


def out_projection(a1, a2, w_stack, layer, resid):
    t, kh = a1.shape
    n = w_stack.shape[-1]
    tm = _pick_tile(t, 1280, ROW_ALIGN)
    tn = _pick_tile(n, 512, LANES)
    return pl.pallas_call(
        _out_proj_kernel,
        grid=(t // tm, n // tn),
        in_specs=[pl.BlockSpec((tm, kh), lambda i, j: (i, 0)),
                  pl.BlockSpec((tm, kh), lambda i, j: (i, 0)),
                  pl.BlockSpec((None, kh, tn), lambda i, j: (layer, 0, j)),
                  pl.BlockSpec((None, kh, tn), lambda i, j: (layer, 1, j)),
                  pl.BlockSpec((tm, tn), lambda i, j: (i, j))],
        out_specs=pl.BlockSpec((tm, tn), lambda i, j: (i, j)),
        out_shape=jax.ShapeDtypeStruct((t, n), F32),
        compiler_params=_params(("parallel", "arbitrary")),
        name="out_projection",
    )(a1, a2, w_stack, w_stack, resid)


def _attn_chunk(q_ref, k_ref, v_ref, tri_ref, carry_ref, acc_ref, h, *, diagonal, first):
    hs = slice(h * HEAD_DIM, (h + 1) * HEAD_DIM)
    q = q_ref[:, hs]
    k = k_ref[:, hs]
    v = v_ref[:, hs]
    z = lax.dot_general(q, k, (((1,), (1,)), ((), ())), preferred_element_type=F32)
    z = z * (HEAD_DIM ** -0.5)
    log_stay = -(jnp.maximum(z, 0.0) + jnp.log(1.0 + jnp.exp(-jnp.abs(z))))
    if diagonal:
        row = lax.broadcasted_iota(jnp.int32, z.shape, 0)
        col = lax.broadcasted_iota(jnp.int32, z.shape, 1)
        visible = col < row
        log_stay = jnp.where(visible, log_stay, 0.0)
    hi, lo = _split_bf16(log_stay)
    sums = jnp.dot(jnp.concatenate([hi, lo], axis=1), tri_ref[...], preferred_element_type=F32)
    expo = z + sums[:, :Q_TILE]
    if not first:
        carry = carry_ref[h]
        expo = expo + carry
    if diagonal:
        expo = jnp.where(visible, expo, MASKED)
    w = jnp.exp(expo).astype(BF16)
    pv = jnp.dot(w, v, preferred_element_type=F32)
    if first:
        acc_ref[:, hs] = pv
        carry_ref[h] = sums[:, Q_TILE:]
    else:
        acc_ref[:, hs] += pv
        carry_ref[h] = carry + sums[:, Q_TILE:]


def _attn_kernel(q_ref, k0_ref, k1_ref, k2_ref, v0_ref, v1_ref, v2_ref, kv_hbm, tri_ref, g_ref,
                 o_ref, carry_ref, acc_ref, kt_ref, vt_ref, flag_ref, sem, *, n_heads, n_qblocks):
    b = pl.program_id(0)
    i = pl.program_id(1)
    width = n_heads * HEAD_DIM
    k_refs = (k0_ref, k1_ref, k2_ref)
    v_refs = (v0_ref, v1_ref, v2_ref)
    kb0 = i - ATTN_WINDOW

    def tail_copies(kb, slot):
        row0 = pl.multiple_of((b * n_qblocks + kb) * Q_TILE, Q_TILE)
        rows = pl.ds(row0, Q_TILE)
        return (pltpu.make_async_copy(kv_hbm.at[rows, pl.ds(width, width)], kt_ref.at[slot], sem.at[0, slot]),
                pltpu.make_async_copy(kv_hbm.at[rows, pl.ds(2 * width, width)], vt_ref.at[slot], sem.at[1, slot]))

    @pl.when(kb0 >= 0)
    def _():
        for cp in tail_copies(kb0, 0):
            cp.start()

    def sweep(k_ref, v_ref, diagonal, first):
        for h in range(n_heads):
            _attn_chunk(q_ref, k_ref, v_ref, tri_ref, carry_ref, acc_ref, h, diagonal=diagonal, first=first)

    sweep(k_refs[0], v_refs[0], True, True)
    for c in range(1, ATTN_WINDOW):
        @pl.when(i >= c)
        def _(c=c):
            sweep(k_refs[c], v_refs[c], False, False)

    def head_flags():
        n = jnp.int32(0)
        for h in range(n_heads):
            f = (jnp.max(carry_ref[h]) > EXP_UNDERFLOW).astype(jnp.int32)
            flag_ref[h] = f
            n = n + f
        return n

    m = carry_ref[0]
    for h in range(1, n_heads):
        m = jnp.maximum(m, carry_ref[h])
    any_left = jnp.logical_and(jnp.max(m) > EXP_UNDERFLOW, kb0 >= 0)

    @pl.when(any_left)
    def _():
        head_flags()

    def tail_cond(state):
        kb, n_active = state
        return jnp.logical_and(kb >= 0, n_active > 0)

    def tail_body(state):
        kb, _ = state
        slot = lax.rem(kb0 - kb, 2)
        for cp in tail_copies(kb, slot):
            cp.wait()

        @pl.when(kb >= 1)
        def _():
            for cp in tail_copies(kb - 1, 1 - slot):
                cp.start()

        for h in range(n_heads):
            @pl.when(flag_ref[h] > 0)
            def _(h=h):
                _attn_chunk(q_ref, kt_ref.at[slot], vt_ref.at[slot], tri_ref, carry_ref, acc_ref, h,
                            diagonal=False, first=False)
        return kb - 1, head_flags()

    kb_end, _ = lax.while_loop(tail_cond, tail_body, (kb0, any_left.astype(jnp.int32)))

    @pl.when(kb_end >= 0)
    def _():
        for cp in tail_copies(kb_end, lax.rem(kb0 - kb_end, 2)):
            cp.wait()

    o_ref[...] = _rms(acc_ref[...], g_ref[...]).astype(o_ref.dtype)


def attention(qkv, gain, batch, n_heads):
    t = qkv.shape[0]
    width = n_heads * HEAD_DIM
    nq = t // batch // Q_TILE
    j = jnp.arange(2 * Q_TILE)[:, None] % Q_TILE
    s = jnp.arange(2 * Q_TILE)[None, :]
    tri = jnp.where(s < Q_TILE, j >= s, True).astype(BF16)

    def qmap(b, i):
        return (b * nq + i, 0)

    def kvmap(c, col):
        return lambda b, i: (b * nq + jnp.maximum(i - c, 0), col)

    blk = (Q_TILE, width)
    return pl.pallas_call(
        functools.partial(_attn_kernel, n_heads=n_heads, n_qblocks=nq),
        grid=(batch, nq),
        in_specs=[pl.BlockSpec(blk, qmap)]
        + [pl.BlockSpec(blk, kvmap(c, 1)) for c in range(ATTN_WINDOW)]
        + [pl.BlockSpec(blk, kvmap(c, 2)) for c in range(ATTN_WINDOW)]
        + [pl.BlockSpec(memory_space=pl.ANY),
           pl.BlockSpec((2 * Q_TILE, 2 * Q_TILE), lambda b, i: (0, 0)),
           pl.BlockSpec((1, width), lambda b, i: (0, 0))],
        out_specs=pl.BlockSpec(blk, qmap),
        out_shape=jax.ShapeDtypeStruct((t, width), BF16),
        scratch_shapes=[pltpu.VMEM((n_heads, Q_TILE, Q_TILE), F32),
                        pltpu.VMEM((Q_TILE, width), F32),
                        pltpu.VMEM((2, Q_TILE, width), BF16),
                        pltpu.VMEM((2, Q_TILE, width), BF16),
                        pltpu.SMEM((n_heads,), jnp.int32),
                        pltpu.SemaphoreType.DMA((2, 2))],
        compiler_params=_params(("parallel", "arbitrary")),
        name="stickbreak_attention",
    )(qkv, qkv, qkv, qkv, qkv, qkv, qkv, qkv, tri, gain.reshape(1, width))


def _ssm_factors(lam_re, lam_im, log_step, b_re, b_im, c_re, c_im):
    depth, g, p, c = b_re.shape
    gp = LANES // c
    jt = g // gp
    lr = lam_re.astype(F32)[:, None]
    li = lam_im.astype(F32)[:, None]
    step = jnp.exp(log_step.astype(F32))[:, None, :, None]
    tau = jnp.arange(SSM_CHUNK + 1, dtype=F32)[None, :, None, None]
    mag = jnp.exp(lr * step * tau)
    ang = li * step * tau
    pw_re = mag * jnp.cos(ang)
    pw_im = mag * jnp.sin(ang)
    abar_re, abar_im = pw_re[:, 1], pw_im[:, 1]
    lr, li = lr[:, 0], li[:, 0]
    den = lr * lr + li * li
    num_re = abar_re - 1.0
    coef_re = ((num_re * lr + abar_im * li) / den)[..., None]
    coef_im = ((abar_im * lr - num_re * li) / den)[..., None]
    br = b_re.astype(F32)
    bi = b_im.astype(F32)
    bbar_re = coef_re * br - coef_im * bi
    bbar_im = coef_re * bi + coef_im * br
    eye = jnp.eye(gp, dtype=F32)

    def expand_b(x):
        x = x.reshape(depth, jt, gp, p, c).transpose(0, 1, 2, 4, 3)
        x = x[:, :, :, :, None, :] * eye[None, None, :, None, :, None]
        return x.reshape(depth, jt, gp * c, gp * p)

    def expand_c(x):
        x = x.astype(F32).reshape(depth, jt, gp, c, p)
        x = x[:, :, :, :, None, :] * eye[None, None, :, None, :, None]
        return x.reshape(depth, jt, gp * c, gp * p)

    def lane_tiles(x):
        return x.reshape(depth, SSM_CHUNK + 1, jt, gp * p).transpose(0, 2, 1, 3)

    return (expand_b(bbar_re), expand_b(bbar_im), expand_c(c_re), expand_c(c_im),
            lane_tiles(pw_re), lane_tiles(pw_im))


def _ssm_build_kernel(bbr_ref, bbi_ref, ctr_ref, cti_ref, pwr_ref, pwi_ref, m_ref, win_ref, wout_ref):
    tc = SSM_CHUNK
    bb_re = bbr_ref[...]
    bb_im = bbi_ref[...]
    ct_re = ctr_ref[...]
    ct_im = cti_ref[...]
    half = bb_re.shape[1]
    nt = (((1,), (1,)), ((), ()))

    def dot_nt(a, b):
        ah, al = _split_bf16(a)
        bh, bl = _split_bf16(b)
        return (lax.dot_general(ah, bh, nt, preferred_element_type=F32)
                + lax.dot_general(al, bh, nt, preferred_element_type=F32)
                + lax.dot_general(ah, bl, nt, preferred_element_type=F32))

    def rows(n):
        return slice(n * LANES, (n + 1) * LANES)

    zero = jnp.zeros((LANES, LANES), BF16)
    for s in range(tc):
        for r in range(s):
            m_ref[rows(s), rows(r)] = zero
    for tau in range(tc + 1):
        pr = pwr_ref[tau:tau + 1, :]
        pi = pwi_ref[tau:tau + 1, :]
        cp_re = ct_re * pr - ct_im * pi
        cp_im = ct_re * pi + ct_im * pr
        if tau < tc:
            lag_block = (dot_nt(bb_re, cp_re) - dot_nt(bb_im, cp_im)).astype(BF16)
            for s in range(tc - tau):
                m_ref[rows(s), rows(s + tau)] = lag_block
            s = tc - 1 - tau
            win_ref[rows(s), :half] = (bb_re * pr - bb_im * pi).astype(BF16)
            win_ref[rows(s), half:] = (bb_re * pi + bb_im * pr).astype(BF16)
        if tau >= 1:
            wout_ref[rows(tau - 1), :half] = cp_re.astype(BF16)
            wout_ref[rows(tau - 1), half:] = (-cp_im).astype(BF16)


def ssm_operators(factors):
    bb_re, bb_im, ct_re, ct_im, pw_re, pw_im = factors
    depth, jt, cw, sw = bb_re.shape
    kdim = SSM_CHUNK * LANES
    fac = pl.BlockSpec((None, None, cw, sw), lambda d, j: (d, j, 0, 0))
    pws = pl.BlockSpec((None, None, SSM_CHUNK + 1, sw), lambda d, j: (d, j, 0, 0))
    out = pl.BlockSpec((None, None, kdim, kdim), lambda d, j: (d, j, 0, 0))
    assert 2 * sw == kdim
    shape = jax.ShapeDtypeStruct((depth, jt, kdim, kdim), BF16)
    return pl.pallas_call(
        _ssm_build_kernel,
        grid=(depth, jt),
        in_specs=[fac, fac, fac, fac, pws, pws],
        out_specs=[out, out, out],
        out_shape=[shape, shape, shape],
        compiler_params=_params(("parallel", "parallel")),
        name="ssm_operators",
    )(bb_re, bb_im, ct_re, ct_im, pw_re, pw_im)


def _gelu_tanh(x):
    return 0.5 * x * (1.0 + jnp.tanh(math.sqrt(2.0 / math.pi) * (x + 0.044715 * (x * x * x))))


def _ssm_kernel(u_ref, m_ref, win_ref, wout_ref, pwr_ref, pwi_ref, d_ref, o_ref, e_ref, sprev_ref, *, n_chunks):
    tc = SSM_CHUNK
    half = pwr_ref.shape[-1]
    uv = jnp.concatenate([u_ref[pl.ds(r, n_chunks, stride=tc), :] for r in range(tc)], axis=1)
    uvb = uv.astype(BF16)
    e_ref[...] = jnp.dot(uvb, win_ref[...], preferred_element_type=F32)
    a_re = pwr_ref[tc:tc + 1, :]
    a_im = pwi_ref[tc:tc + 1, :]

    def step(n, state):
        s_re, s_im = state
        sprev_ref[pl.ds(n, 1), :half] = s_re
        sprev_ref[pl.ds(n, 1), half:] = s_im
        e = e_ref[pl.ds(n, 1), :]
        return (a_re * s_re - a_im * s_im + e[:, :half],
                a_re * s_im + a_im * s_re + e[:, half:])

    zero = jnp.zeros((1, half), F32)
    lax.fori_loop(0, n_chunks, step, (zero, zero))
    y = jnp.dot(uvb, m_ref[...], preferred_element_type=F32)
    y += lax.dot_general(sprev_ref[...].astype(BF16), wout_ref[...], (((1,), (1,)), ((), ())),
                         preferred_element_type=F32)
    d = d_ref[...]
    for r in range(tc):
        cols = slice(r * LANES, (r + 1) * LANES)
        o_ref[pl.ds(r, n_chunks, stride=tc), :] = _gelu_tanh(y[:, cols] + d * uv[:, cols])


def ssm_scan(u, ops, pw, layer, d_skip, batch):
    m_intra, w_in, w_out = ops
    pw_re, pw_im = pw
    t, width = u.shape
    lp = t // batch
    jt = width // LANES
    nc = lp // SSM_CHUNK
    kdim = SSM_CHUNK * LANES
    sw = pw_re.shape[-1]
    op = pl.BlockSpec((None, None, kdim, kdim), lambda j, b: (layer, j, 0, 0))
    pws = pl.BlockSpec((None, None, SSM_CHUNK + 1, sw), lambda j, b: (layer, j, 0, 0))
    return pl.pallas_call(
        functools.partial(_ssm_kernel, n_chunks=nc),
        grid=(jt, batch),
        in_specs=[pl.BlockSpec((lp, LANES), lambda j, b: (b, j)), op, op, op, pws, pws,
                  pl.BlockSpec((1, LANES), lambda j, b: (0, j))],
        out_specs=pl.BlockSpec((lp, LANES), lambda j, b: (b, j)),
        out_shape=jax.ShapeDtypeStruct((t, width), F32),
        scratch_shapes=[pltpu.VMEM((nc, 2 * sw), F32), pltpu.VMEM((nc, 2 * sw), F32)],
        compiler_params=_params(("parallel", "arbitrary")),
        name="ssm_scan",
    )(u, m_intra, w_in, w_out, pw_re, pw_im, d_skip.reshape(1, width))


def _glu_kernel(g_ref, w_ref, b_ref, gain_ref, o_ref):
    g = g_ref[...]
    zz = jnp.dot(g.astype(BF16), w_ref[...], preferred_element_type=F32) + b_ref[...]
    y = g * (1.0 / (1.0 + jnp.exp(-zz)))
    o_ref[...] = _rms(y, gain_ref[...]).astype(o_ref.dtype)


def glu_norm(g, w_stack, layer, bias, gain):
    t, width = g.shape
    tm = _pick_tile(t, 640, ROW_ALIGN)
    return pl.pallas_call(
        _glu_kernel,
        grid=(t // tm,),
        in_specs=[pl.BlockSpec((tm, width), lambda i: (i, 0)),
                  pl.BlockSpec((None, width, width), lambda i: (layer, 0, 0)),
                  pl.BlockSpec((1, width), lambda i: (0, 0)),
                  pl.BlockSpec((1, width), lambda i: (0, 0))],
        out_specs=pl.BlockSpec((tm, width), lambda i: (i, 0)),
        out_shape=jax.ShapeDtypeStruct((t, width), BF16),
        compiler_params=_params(("parallel",)),
        name="glu_norm",
    )(g, w_stack, bias.reshape(1, width), gain.reshape(1, width))


def _route_kernel(h_ref, gain_ref, wh_ref, wl_ref, bias_ref, hn_ref, eid_ref, gate_ref, *, n_groups, per_group):
    hn = _rms(h_ref[...], gain_ref[...])
    hn_ref[...] = hn
    xh, xl = _split_bf16(hn)
    wh = wh_ref[...]
    logits = (jnp.dot(xh, wh, preferred_element_type=F32) + jnp.dot(xl, wh, preferred_element_type=F32)
              + jnp.dot(xh, wl_ref[...], preferred_element_type=F32) + bias_ref[...])
    lane = lax.broadcasted_iota(jnp.int32, logits.shape, 1)

    def first_lane_of_max(vals, top):
        return jnp.min(jnp.where(vals == top, lane, LANES), axis=-1, keepdims=True)

    gmask = lane < n_groups
    lg = jnp.where(gmask, logits, MASKED)
    g_top = jnp.max(lg, axis=-1, keepdims=True)
    g_prob = 1.0 / jnp.sum(jnp.where(gmask, jnp.exp(lg - g_top), 0.0), axis=-1, keepdims=True)
    g_idx = first_lane_of_max(lg, g_top)
    lo_lane = n_groups + g_idx * per_group
    emask = jnp.logical_and(lane >= lo_lane, lane < lo_lane + per_group)
    le = jnp.where(emask, logits, MASKED)
    m1 = jnp.max(le, axis=-1, keepdims=True)
    i1 = first_lane_of_max(le, m1)
    denom = jnp.sum(jnp.where(emask, jnp.exp(le - m1), 0.0), axis=-1, keepdims=True)
    le2 = jnp.where(lane == i1, MASKED, le)
    m2 = jnp.max(le2, axis=-1, keepdims=True)
    i2 = first_lane_of_max(le2, m2)
    p1 = 1.0 / denom
    p2 = jnp.exp(m2 - m1) / denom
    psum = p1 + p2
    eid_ref[...] = jnp.where(lane == 0, i1 - n_groups, jnp.where(lane == 1, i2 - n_groups, 0))
    gate_ref[...] = jnp.where(lane == 0, g_prob * (p1 / psum), jnp.where(lane == 1, g_prob * (p2 / psum), 0.0))


def norm_route(h, gain, w_router, b_router, n_groups, per_group):
    t, d = h.shape
    tm = _pick_tile(t, 256, ROW_ALIGN)
    n_out = w_router.shape[-1]
    w_pad = jnp.zeros((d, LANES), F32).at[:, :n_out].set(w_router)
    wh, wl = _split_bf16(w_pad)
    bias = jnp.zeros((1, LANES), F32).at[0, :n_out].set(b_router)
    return pl.pallas_call(
        functools.partial(_route_kernel, n_groups=n_groups, per_group=per_group),
        grid=(t // tm,),
        in_specs=[pl.BlockSpec((tm, d), lambda i: (i, 0)),
                  pl.BlockSpec((1, d), lambda i: (0, 0)),
                  pl.BlockSpec((d, LANES), lambda i: (0, 0)),
                  pl.BlockSpec((d, LANES), lambda i: (0, 0)),
                  pl.BlockSpec((1, LANES), lambda i: (0, 0))],
        out_specs=[pl.BlockSpec((tm, d), lambda i: (i, 0)),
                   pl.BlockSpec((tm, LANES), lambda i: (i, 0)),
                   pl.BlockSpec((tm, LANES), lambda i: (i, 0))],
        out_shape=[jax.ShapeDtypeStruct((t, d), F32),
                   jax.ShapeDtypeStruct((t, LANES), jnp.int32),
                   jax.ShapeDtypeStruct((t, LANES), F32)],
        compiler_params=_params(("parallel",)),
        name="norm_route",
    )(h, gain.reshape(1, d), wh, wl, bias)


def dispatch_plan(eid, n_experts):
    n_tok, top_k = eid.shape
    n_assign = n_tok * top_k
    n_blocks = -(-n_assign // EXPERT_ROWS) + n_experts
    expert = eid.reshape(-1)
    iota = jnp.arange(n_assign, dtype=jnp.int32)
    experts = jnp.arange(n_experts, dtype=jnp.int32)
    expert_sorted, order = lax.sort((expert, iota), num_keys=1, is_stable=True)
    counts = jnp.sum((expert[:, None] == experts[None, :]).astype(jnp.int32), axis=0)
    padded = (counts + EXPERT_ROWS - 1) // EXPERT_ROWS * EXPERT_ROWS
    pad_end = jnp.cumsum(padded)
    pad_start = pad_end - padded
    start = jnp.cumsum(counts) - counts
    shift = pad_start - start
    dshift = shift - jnp.concatenate([jnp.zeros((1,), jnp.int32), shift[:-1]])
    dest = iota + jnp.sum(jnp.where(expert_sorted[:, None] >= experts[None, :], dshift[None, :], 0), axis=1)
    _, pos = lax.sort((order, dest), num_keys=1)
    blk_row0 = jnp.arange(n_blocks, dtype=jnp.int32) * EXPERT_ROWS
    block_expert = jnp.minimum(
        jnp.sum((pad_end[None, :] <= blk_row0[:, None]).astype(jnp.int32), axis=1), n_experts - 1)
    block_shift = jnp.sum(jnp.where(block_expert[:, None] == experts[None, :], shift[None, :], 0), axis=1)
    block_src = blk_row0 - block_shift
    n_used = (pad_end[-1] // EXPERT_ROWS).astype(jnp.int32).reshape(1)
    pos = pos.reshape(n_tok // COMBINE_ROWS, COMBINE_ROWS, top_k).transpose(0, 2, 1).reshape(-1)
    return order // top_k, block_expert, block_src, n_used, pos


def _gather_rows(src_hbm, dst_ref, sem, idx_ref, base, n_rows, limit=None):
    def body(r, c):
        at = base + r
        if limit is not None:
            at = jnp.minimum(at, limit)
        row = idx_ref[at]
        pltpu.make_async_copy(src_hbm.at[pl.ds(row, 1), :], dst_ref.at[pl.ds(r, 1), :], sem).start()
        return c
    lax.fori_loop(0, n_rows, body, 0, unroll=GATHER_UNROLL)


def _wait_rows(src_hbm, dst_ref, sem, n_rows):
    pltpu.make_async_copy(src_hbm.at[pl.ds(0, n_rows), :], dst_ref, sem).wait()


def _moe_kernel(be_ref, bs_ref, nu_ref, tok_ref, x_hbm, wg_ref, wu_ref, wd_ref, o_ref, xbuf, sem):
    i = pl.program_id(0)
    n_used = nu_ref[0]
    slot = lax.rem(i, 2)
    last = tok_ref.shape[0] - 1

    def fetch(blk, s):
        _gather_rows(x_hbm, xbuf.at[s], sem.at[s], tok_ref, bs_ref[blk], EXPERT_ROWS, limit=last)

    @pl.when(jnp.logical_and(i == 0, n_used > 0))
    def _():
        fetch(0, 0)

    @pl.when(i + 1 < n_used)
    def _():
        fetch(i + 1, 1 - slot)

    @pl.when(i < n_used)
    def _():
        _wait_rows(x_hbm, xbuf.at[slot], sem.at[slot], EXPERT_ROWS)
        x = xbuf[slot].astype(BF16)
        hg = jnp.dot(x, wg_ref[...], preferred_element_type=F32)
        hu = jnp.dot(x, wu_ref[...], preferred_element_type=F32)
        hid = hg * (1.0 / (1.0 + jnp.exp(-hg))) * hu
        o_ref[...] = jnp.dot(hid.astype(BF16), wd_ref[...], preferred_element_type=F32)

    @pl.when(i >= n_used)
    def _():
        o_ref[...] = jnp.zeros_like(o_ref)


def expert_blocks(hn, plan, w_gate, w_up, w_down, layer):
    tok_sorted, block_expert, block_src, n_used, _ = plan
    d = hn.shape[1]
    f = w_gate.shape[-1]
    n_blocks = block_expert.shape[0]
    n_rows = n_blocks * EXPERT_ROWS

    def wmap(i, be, bs, nu, tok):
        return (layer, be[i], 0, 0)

    return pl.pallas_call(
        _moe_kernel,
        grid_spec=pltpu.PrefetchScalarGridSpec(
            num_scalar_prefetch=4,
            grid=(n_blocks,),
            in_specs=[pl.BlockSpec(memory_space=pl.ANY),
                      pl.BlockSpec((None, None, d, f), wmap),
                      pl.BlockSpec((None, None, d, f), wmap),
                      pl.BlockSpec((None, None, f, d), wmap)],
            out_specs=pl.BlockSpec((EXPERT_ROWS, d), lambda i, be, bs, nu, tok: (i, 0)),
            scratch_shapes=[pltpu.VMEM((2, EXPERT_ROWS, d), F32), pltpu.SemaphoreType.DMA((2,))]),
        out_shape=jax.ShapeDtypeStruct((n_rows, d), F32),
        compiler_params=_params(("arbitrary",)),
        name="expert_blocks",
    )(block_expert, block_src, n_used, tok_sorted, hn, w_gate, w_up, w_down)


def _combine_kernel(pos_ref, y_hbm, h_ref, gate_ref, gain_ref, oh_ref, on_ref, ybuf, sem, *, top_k):
    i = pl.program_id(0)
    n = pl.num_programs(0)
    slot = lax.rem(i, 2)
    rows = COMBINE_ROWS * top_k

    def fetch(blk, s):
        _gather_rows(y_hbm, ybuf.at[s], sem.at[s], pos_ref, blk * rows, rows)

    @pl.when(i == 0)
    def _():
        fetch(0, 0)

    @pl.when(i + 1 < n)
    def _():
        fetch(i + 1, 1 - slot)

    _wait_rows(y_hbm, ybuf.at[slot], sem.at[slot], rows)
    h_new = h_ref[...]
    for k in range(top_k):
        h_new = h_new + gate_ref[:, k:k + 1] * ybuf[slot, k * COMBINE_ROWS:(k + 1) * COMBINE_ROWS, :]
    oh_ref[...] = h_new
    on_ref[...] = _rms(h_new, gain_ref[...]).astype(on_ref.dtype)


def combine(h, y_rows, pos, gate, gain, norm_dtype, top_k):
    t, d = h.shape
    return pl.pallas_call(
        functools.partial(_combine_kernel, top_k=top_k),
        grid_spec=pltpu.PrefetchScalarGridSpec(
            num_scalar_prefetch=1,
            grid=(t // COMBINE_ROWS,),
            in_specs=[pl.BlockSpec(memory_space=pl.ANY),
                      pl.BlockSpec((COMBINE_ROWS, d), lambda i, pos: (i, 0)),
                      pl.BlockSpec((COMBINE_ROWS, LANES), lambda i, pos: (i, 0)),
                      pl.BlockSpec((1, d), lambda i, pos: (0, 0))],
            out_specs=[pl.BlockSpec((COMBINE_ROWS, d), lambda i, pos: (i, 0)),
                       pl.BlockSpec((COMBINE_ROWS, d), lambda i, pos: (i, 0))],
            scratch_shapes=[pltpu.VMEM((2, COMBINE_ROWS * top_k, d), F32), pltpu.SemaphoreType.DMA((2,))]),
        out_shape=[jax.ShapeDtypeStruct((t, d), F32), jax.ShapeDtypeStruct((t, d), norm_dtype)],
        compiler_params=_params(("arbitrary",)),
        name="combine",
    )(pos, y_rows, h, gate, gain.reshape(1, d))


def kernel(x, meta_tokens, ln_mix, w_in, ssm_lam_re, ssm_lam_im, ssm_log_step, ssm_b_re, ssm_b_im, ssm_c_re, ssm_c_im, ssm_d, w_glu, b_glu, g_attn_out, g_ssm_out, w_out, ln_ffn, w_route_group, b_route_group, w_route_expert, b_route_expert, w_gate, w_up, w_down, ln_final):
    batch, seq, d_model = x.shape
    depth = w_in.shape[0]
    n_meta = meta_tokens.shape[0]
    ssm_width = ssm_d.shape[-1]
    attn_width = g_attn_out.shape[-1]
    n_heads = attn_width // HEAD_DIM
    n_groups = w_route_group.shape[-1]
    n_experts = w_route_expert.shape[-1]
    top_k = 2
    length = n_meta + seq
    lp = -(-length // ROW_ALIGN) * ROW_ALIGN
    t = batch * lp
    assert attn_width == ssm_width

    meta = jnp.broadcast_to(meta_tokens.astype(x.dtype)[None], (batch, n_meta, d_model))
    h = jnp.concatenate([meta, x, jnp.zeros((batch, lp - length, d_model), x.dtype)], axis=1).reshape(t, d_model)

    w_in_b = w_in.astype(BF16)
    w_glu_b = w_glu.astype(BF16)
    w_out_b = w_out.astype(BF16)
    w_gate_b = w_gate.astype(BF16)
    w_up_b = w_up.astype(BF16)
    w_down_b = w_down.astype(BF16)
    factors = _ssm_factors(ssm_lam_re, ssm_lam_im, ssm_log_step, ssm_b_re, ssm_b_im, ssm_c_re, ssm_c_im)
    ssm_ops = ssm_operators(factors)

    hn = rmsnorm(h, ln_mix[0], BF16)
    for l in range(depth):
        qkv = in_projection(hn, w_in_b, l, 0, 3 * attn_width, BF16)
        u = in_projection(hn, w_in_b, l, 3 * attn_width, ssm_width, F32)
        attn_n = attention(qkv, g_attn_out[l], batch, n_heads)
        g = ssm_scan(u, ssm_ops, factors[4:], l, ssm_d[l], batch)
        ssm_n = glu_norm(g, w_glu_b, l, b_glu[l], g_ssm_out[l])
        h = out_projection(attn_n, ssm_n, w_out_b, l, h)
        w_router = jnp.concatenate([w_route_group[l], w_route_expert[l]], axis=-1)
        b_router = jnp.concatenate([b_route_group[l], b_route_expert[l]], axis=-1)
        hn2, eid, gate = norm_route(h, ln_ffn[l], w_router, b_router, n_groups, n_experts // n_groups)
        plan = dispatch_plan(eid[:, :top_k], n_experts)
        y_rows = expert_blocks(hn2, plan, w_gate_b, w_up_b, w_down_b, l)
        last = l == depth - 1
        next_gain = ln_final if last else ln_mix[l + 1]
        h, hn = combine(h, y_rows, plan[4], gate, next_gain, F32 if last else BF16, top_k)
    return hn.reshape(batch, lp, d_model)[:, n_meta:length]
```

```python
import functools
import math

import jax
import jax.numpy as jnp
from jax import lax
from jax.experimental import pallas as pl
from jax.experimental.pallas import tpu as pltpu

F32 = jnp.float32
BF16 = jnp.bfloat16

LANES = 128
HEAD_DIM = 128
Q_TILE = 128
ROW_ALIGN = 128
ATTN_WINDOW = 3
SSM_CHUNK = 8
EXPERT_ROWS = 256
COMBINE_ROWS = 128
GATHER_UNROLL = 8
NORM_EPS = 1e-6
EXP_UNDERFLOW = -151.0
MASKED = -1e30
VMEM_LIMIT = 56 * 1024 * 1024


def _pick_tile(n, target, align):
    best = align
    for t in range(align, min(n, target) + 1, align):
        if n % t == 0:
            best = t
    return best


def _params(sem, vmem=VMEM_LIMIT):
    return pltpu.CompilerParams(dimension_semantics=sem, vmem_limit_bytes=vmem)


def _rms(x, gain):
    return x * lax.rsqrt(jnp.mean(x * x, axis=-1, keepdims=True) + NORM_EPS) * gain


def _split_bf16(x):
    hi = x.astype(BF16)
    return hi, (x - hi.astype(F32)).astype(BF16)


def _norm_kernel(x_ref, g_ref, o_ref):
    o_ref[...] = _rms(x_ref[...], g_ref[...]).astype(o_ref.dtype)


def rmsnorm(x, gain, out_dtype):
    t, d = x.shape
    tm = _pick_tile(t, 512, ROW_ALIGN)
    return pl.pallas_call(
        _norm_kernel,
        grid=(t // tm,),
        in_specs=[pl.BlockSpec((tm, d), lambda i: (i, 0)),
                  pl.BlockSpec((1, d), lambda i: (0, 0))],
        out_specs=pl.BlockSpec((tm, d), lambda i: (i, 0)),
        out_shape=jax.ShapeDtypeStruct((t, d), out_dtype),
        compiler_params=_params(("parallel",)),
        name="rmsnorm",
    )(x, gain.reshape(1, d))


def _mm_kernel(a_ref, w_ref, o_ref, *, scaled_tiles, scale):
    acc = jnp.dot(a_ref[...], w_ref[...], preferred_element_type=F32)
    if scaled_tiles:
        acc = acc * jnp.where(pl.program_id(1) < scaled_tiles, scale, 1.0)
    o_ref[...] = acc.astype(o_ref.dtype)


def in_projection(a, w_stack, layer, col0, ncols, out_dtype, scaled_cols=0, scale=1.0):
    t, k = a.shape
    tm = _pick_tile(t, 1280, ROW_ALIGN)
    tn = _pick_tile(math.gcd(ncols, scaled_cols), 512, LANES)
    assert col0 % tn == 0
    return pl.pallas_call(
        functools.partial(_mm_kernel, scaled_tiles=scaled_cols // tn, scale=scale),
        grid=(t // tm, ncols // tn),
        in_specs=[pl.BlockSpec((tm, k), lambda i, j: (i, 0)),
                  pl.BlockSpec((None, k, tn), lambda i, j: (layer, 0, col0 // tn + j))],
        out_specs=pl.BlockSpec((tm, tn), lambda i, j: (i, j)),
        out_shape=jax.ShapeDtypeStruct((t, ncols), out_dtype),
        compiler_params=_params(("parallel", "arbitrary")),
        name="in_projection",
    )(a, w_stack)


def _out_proj_kernel(a1_ref, a2_ref, w1_ref, w2_ref, r_ref, o_ref):
    acc = jnp.dot(a1_ref[...], w1_ref[...], preferred_element_type=F32)
    acc += jnp.dot(a2_ref[...], w2_ref[...], preferred_element_type=F32)
    o_ref[...] = r_ref[...] + acc


def out_projection(a1, a2, w_stack, layer, resid):
    t, kh = a1.shape
    n = w_stack.shape[-1]
    tm = _pick_tile(t, 1280, ROW_ALIGN)
    tn = _pick_tile(n, 512, LANES)
    return pl.pallas_call(
        _out_proj_kernel,
        grid=(t // tm, n // tn),
        in_specs=[pl.BlockSpec((tm, kh), lambda i, j: (i, 0)),
                  pl.BlockSpec((tm, kh), lambda i, j: (i, 0)),
                  pl.BlockSpec((None, kh, tn), lambda i, j: (layer, 0, j)),
                  pl.BlockSpec((None, kh, tn), lambda i, j: (layer, 1, j)),
                  pl.BlockSpec((tm, tn), lambda i, j: (i, j))],
        out_specs=pl.BlockSpec((tm, tn), lambda i, j: (i, j)),
        out_shape=jax.ShapeDtypeStruct((t, n), F32),
        compiler_params=_params(("parallel", "arbitrary")),
        name="out_projection",
    )(a1, a2, w_stack, w_stack, resid)


def _log2_leave(z2):
    return jnp.maximum(z2, 0.0) + jnp.log2(1.0 + jnp.exp2(-jnp.abs(z2)))


def _attn_chunk(q_ref, k_ref, v_ref, tri_ref, carry_ref, acc_ref, h):
    hs = slice(h * HEAD_DIM, (h + 1) * HEAD_DIM)
    z2 = lax.dot_general(q_ref[:, hs], k_ref[:, hs], (((1,), (1,)), ((), ())), preferred_element_type=F32)
    hi, lo = _split_bf16(_log2_leave(z2))
    sums = jnp.dot(jnp.concatenate([hi, lo], axis=1), tri_ref[...], preferred_element_type=F32)
    carry = carry_ref[h]
    w = jnp.exp2(z2 + sums[:, :Q_TILE] + carry).astype(BF16)
    acc_ref[:, hs] += jnp.dot(w, v_ref[:, hs], preferred_element_type=F32)
    carry_ref[h] = carry + sums[:, Q_TILE:]


def _attn_window(q_ref, k_refs, v_refs, tri_ref, carry_ref, acc_ref, z_ref, hl_ref, i, n_heads):
    shape = (Q_TILE, Q_TILE)
    visible = lax.broadcasted_iota(jnp.int32, shape, 1) < lax.broadcasted_iota(jnp.int32, shape, 0)
    nt = (((1,), (1,)), ((), ()))
    for c in range(ATTN_WINDOW):
        for h in range(n_heads):
            hs = slice(h * HEAD_DIM, (h + 1) * HEAD_DIM)
            z2 = lax.dot_general(q_ref[:, hs], k_refs[c][:, hs], nt, preferred_element_type=F32)
            z_ref[c, h] = z2
            leave = _log2_leave(z2)
            if c == 0:
                leave = jnp.where(visible, leave, 0.0)
            hi, lo = _split_bf16(leave)
            hl_ref[c, h, :, :Q_TILE] = hi
            hl_ref[c, h, :, Q_TILE:] = lo
    for c in range(ATTN_WINDOW):
        tri = tri_ref[...]
        if c > 0:
            present = i >= c
            tri = jnp.where(present, tri, jnp.zeros_like(tri))
        for h in range(n_heads):
            hs = slice(h * HEAD_DIM, (h + 1) * HEAD_DIM)
            sums = jnp.dot(hl_ref[c, h], tri, preferred_element_type=F32)
            expo = z_ref[c, h] + sums[:, :Q_TILE]
            v = v_refs[c][:, hs]
            if c == 0:
                expo = jnp.where(visible, expo, MASKED)
            else:
                expo = jnp.minimum(expo + carry_ref[h], 0.0)
                v = jnp.where(present, v, jnp.zeros_like(v))
            pv = jnp.dot(jnp.exp2(expo).astype(BF16), v, preferred_element_type=F32)
            if c == 0:
                acc_ref[:, hs] = pv
                carry_ref[h] = sums[:, Q_TILE:]
            else:
                acc_ref[:, hs] += pv
                carry_ref[h] += sums[:, Q_TILE:]


def _attn_kernel(q_ref, k0_ref, k1_ref, k2_ref, v0_ref, v1_ref, v2_ref, kv_hbm, tri_ref, g_ref,
                 o_ref, carry_ref, acc_ref, z_ref, hl_ref, kt_ref, vt_ref, flag_ref, sem, *, n_heads, n_qblocks):
    b = pl.program_id(0)
    i = pl.program_id(1)
    width = n_heads * HEAD_DIM
    k_refs = (k0_ref, k1_ref, k2_ref)
    v_refs = (v0_ref, v1_ref, v2_ref)
    kb0 = i - ATTN_WINDOW

    def tail_copies(kb, slot):
        row0 = pl.multiple_of((b * n_qblocks + kb) * Q_TILE, Q_TILE)
        rows = pl.ds(row0, Q_TILE)
        return (pltpu.make_async_copy(kv_hbm.at[rows, pl.ds(width, width)], kt_ref.at[slot], sem.at[0, slot]),
                pltpu.make_async_copy(kv_hbm.at[rows, pl.ds(2 * width, width)], vt_ref.at[slot], sem.at[1, slot]))

    @pl.when(kb0 >= 0)
    def _():
        for cp in tail_copies(kb0, 0):
            cp.start()

    _attn_window(q_ref, k_refs, v_refs, tri_ref, carry_ref, acc_ref, z_ref, hl_ref, i, n_heads)

    def head_flags():
        n = jnp.int32(0)
        for h in range(n_heads):
            f = (jnp.max(carry_ref[h]) > EXP_UNDERFLOW).astype(jnp.int32)
            flag_ref[h] = f
            n = n + f
        return n

    m = carry_ref[0]
    for h in range(1, n_heads):
        m = jnp.maximum(m, carry_ref[h])
    any_left = jnp.logical_and(jnp.max(m) > EXP_UNDERFLOW, kb0 >= 0)

    @pl.when(any_left)
    def _():
        head_flags()

    def tail_cond(state):
        kb, n_active = state
        return jnp.logical_and(kb >= 0, n_active > 0)

    def tail_body(state):
        kb, _ = state
        slot = lax.rem(kb0 - kb, 2)
        for cp in tail_copies(kb, slot):
            cp.wait()

        @pl.when(kb >= 1)
        def _():
            for cp in tail_copies(kb - 1, 1 - slot):
                cp.start()

        for h in range(n_heads):
            @pl.when(flag_ref[h] > 0)
            def _(h=h):
                _attn_chunk(q_ref, kt_ref.at[slot], vt_ref.at[slot], tri_ref, carry_ref, acc_ref, h)
        return kb - 1, head_flags()

    kb_end, _ = lax.while_loop(tail_cond, tail_body, (kb0, any_left.astype(jnp.int32)))

    @pl.when(kb_end >= 0)
    def _():
        for cp in tail_copies(kb_end, lax.rem(kb0 - kb_end, 2)):
            cp.wait()

    o_ref[...] = _rms(acc_ref[...], g_ref[...]).astype(o_ref.dtype)


def attention(qkv, gain, batch, n_heads):
    t = qkv.shape[0]
    width = n_heads * HEAD_DIM
    nq = t // batch // Q_TILE
    j = jnp.arange(2 * Q_TILE)[:, None] % Q_TILE
    s = jnp.arange(2 * Q_TILE)[None, :]
    tri = -jnp.where(s < Q_TILE, j >= s, True).astype(BF16)

    def qmap(b, i):
        return (b * nq + i, 0)

    def kvmap(c, col):
        return lambda b, i: (b * nq + jnp.maximum(i - c, 0), col)

    blk = (Q_TILE, width)
    return pl.pallas_call(
        functools.partial(_attn_kernel, n_heads=n_heads, n_qblocks=nq),
        grid=(batch, nq),
        in_specs=[pl.BlockSpec(blk, qmap)]
        + [pl.BlockSpec(blk, kvmap(c, 1)) for c in range(ATTN_WINDOW)]
        + [pl.BlockSpec(blk, kvmap(c, 2)) for c in range(ATTN_WINDOW)]
        + [pl.BlockSpec(memory_space=pl.ANY),
           pl.BlockSpec((2 * Q_TILE, 2 * Q_TILE), lambda b, i: (0, 0)),
           pl.BlockSpec((1, width), lambda b, i: (0, 0))],
        out_specs=pl.BlockSpec(blk, qmap),
        out_shape=jax.ShapeDtypeStruct((t, width), BF16),
        scratch_shapes=[pltpu.VMEM((n_heads, Q_TILE, Q_TILE), F32),
                        pltpu.VMEM((Q_TILE, width), F32),
                        pltpu.VMEM((ATTN_WINDOW, n_heads, Q_TILE, Q_TILE), F32),
                        pltpu.VMEM((ATTN_WINDOW, n_heads, Q_TILE, 2 * Q_TILE), BF16),
                        pltpu.VMEM((2, Q_TILE, width), BF16),
                        pltpu.VMEM((2, Q_TILE, width), BF16),
                        pltpu.SMEM((n_heads,), jnp.int32),
                        pltpu.SemaphoreType.DMA((2, 2))],
        compiler_params=_params(("parallel", "arbitrary")),
        name="stickbreak_attention",
    )(qkv, qkv, qkv, qkv, qkv, qkv, qkv, qkv, tri, gain.reshape(1, width))


def _ssm_factors(lam_re, lam_im, log_step, b_re, b_im, c_re, c_im):
    depth, g, p, c = b_re.shape
    gp = LANES // c
    jt = g // gp
    lr = lam_re.astype(F32)[:, None]
    li = lam_im.astype(F32)[:, None]
    step = jnp.exp(log_step.astype(F32))[:, None, :, None]
    tau = jnp.arange(SSM_CHUNK + 1, dtype=F32)[None, :, None, None]
    mag = jnp.exp(lr * step * tau)
    ang = li * step * tau
    pw_re = mag * jnp.cos(ang)
    pw_im = mag * jnp.sin(ang)
    abar_re, abar_im = pw_re[:, 1], pw_im[:, 1]
    lr, li = lr[:, 0], li[:, 0]
    den = lr * lr + li * li
    num_re = abar_re - 1.0
    coef_re = ((num_re * lr + abar_im * li) / den)[..., None]
    coef_im = ((abar_im * lr - num_re * li) / den)[..., None]
    br = b_re.astype(F32)
    bi = b_im.astype(F32)
    bbar_re = coef_re * br - coef_im * bi
    bbar_im = coef_re * bi + coef_im * br
    eye = jnp.eye(gp, dtype=F32)

    def expand_b(x):
        x = x.reshape(depth, jt, gp, p, c).transpose(0, 1, 2, 4, 3)
        x = x[:, :, :, :, None, :] * eye[None, None, :, None, :, None]
        return x.reshape(depth, jt, gp * c, gp * p)

    def expand_c(x):
        x = x.astype(F32).reshape(depth, jt, gp, c, p)
        x = x[:, :, :, :, None, :] * eye[None, None, :, None, :, None]
        return x.reshape(depth, jt, gp * c, gp * p)

    def lane_tiles(x):
        return x.reshape(depth, SSM_CHUNK + 1, jt, gp * p).transpose(0, 2, 1, 3)

    return (expand_b(bbar_re), expand_b(bbar_im), expand_c(c_re), expand_c(c_im),
            lane_tiles(pw_re), lane_tiles(pw_im))


def _ssm_build_kernel(bbr_ref, bbi_ref, ctr_ref, cti_ref, pwr_ref, pwi_ref, m_ref, win_ref, wout_ref):
    tc = SSM_CHUNK
    bb_re = bbr_ref[...]
    bb_im = bbi_ref[...]
    ct_re = ctr_ref[...]
    ct_im = cti_ref[...]
    half = bb_re.shape[1]
    nt = (((1,), (1,)), ((), ()))

    def dot_nt(a, b):
        ah, al = _split_bf16(a)
        bh, bl = _split_bf16(b)
        return (lax.dot_general(ah, bh, nt, preferred_element_type=F32)
                + lax.dot_general(al, bh, nt, preferred_element_type=F32)
                + lax.dot_general(ah, bl, nt, preferred_element_type=F32))

    def rows(n):
        return slice(n * LANES, (n + 1) * LANES)

    zero = jnp.zeros((LANES, LANES), BF16)
    for s in range(tc):
        for r in range(s):
            m_ref[rows(s), rows(r)] = zero
    for tau in range(tc + 1):
        pr = pwr_ref[tau:tau + 1, :]
        pi = pwi_ref[tau:tau + 1, :]
        cp_re = ct_re * pr - ct_im * pi
        cp_im = ct_re * pi + ct_im * pr
        if tau < tc:
            lag_block = (dot_nt(bb_re, cp_re) - dot_nt(bb_im, cp_im)).astype(BF16)
            for s in range(tc - tau):
                m_ref[rows(s), rows(s + tau)] = lag_block
            s = tc - 1 - tau
            win_ref[rows(s), :half] = (bb_re * pr - bb_im * pi).astype(BF16)
            win_ref[rows(s), half:] = (bb_re * pi + bb_im * pr).astype(BF16)
        if tau >= 1:
            wout_ref[rows(tau - 1), :half] = cp_re.astype(BF16)
            wout_ref[rows(tau - 1), half:] = (-cp_im).astype(BF16)


def ssm_operators(factors):
    bb_re, bb_im, ct_re, ct_im, pw_re, pw_im = factors
    depth, jt, cw, sw = bb_re.shape
    kdim = SSM_CHUNK * LANES
    fac = pl.BlockSpec((None, None, cw, sw), lambda d, j: (d, j, 0, 0))
    pws = pl.BlockSpec((None, None, SSM_CHUNK + 1, sw), lambda d, j: (d, j, 0, 0))
    out = pl.BlockSpec((None, None, kdim, kdim), lambda d, j: (d, j, 0, 0))
    assert 2 * sw == kdim
    shape = jax.ShapeDtypeStruct((depth, jt, kdim, kdim), BF16)
    return pl.pallas_call(
        _ssm_build_kernel,
        grid=(depth, jt),
        in_specs=[fac, fac, fac, fac, pws, pws],
        out_specs=[out, out, out],
        out_shape=[shape, shape, shape],
        compiler_params=_params(("parallel", "parallel")),
        name="ssm_operators",
    )(bb_re, bb_im, ct_re, ct_im, pw_re, pw_im)


def _gelu_tanh(x):
    return 0.5 * x * (1.0 + jnp.tanh(math.sqrt(2.0 / math.pi) * (x + 0.044715 * (x * x * x))))


def _ssm_kernel(u_ref, m_ref, win_ref, wout_ref, pwr_ref, pwi_ref, d_ref, o_ref, e_ref, sprev_ref, *, n_chunks):
    tc = SSM_CHUNK
    half = pwr_ref.shape[-1]
    uv = jnp.concatenate([u_ref[pl.ds(r, n_chunks, stride=tc), :] for r in range(tc)], axis=1)
    uvb = uv.astype(BF16)
    e_ref[...] = jnp.dot(uvb, win_ref[...], preferred_element_type=F32)
    a_re = pwr_ref[tc:tc + 1, :]
    a_im = pwi_ref[tc:tc + 1, :]

    def step(n, state):
        s_re, s_im = state
        sprev_ref[pl.ds(n, 1), :half] = s_re
        sprev_ref[pl.ds(n, 1), half:] = s_im
        e = e_ref[pl.ds(n, 1), :]
        return (a_re * s_re - a_im * s_im + e[:, :half],
                a_re * s_im + a_im * s_re + e[:, half:])

    zero = jnp.zeros((1, half), F32)
    lax.fori_loop(0, n_chunks, step, (zero, zero))
    y = jnp.dot(uvb, m_ref[...], preferred_element_type=F32)
    y += lax.dot_general(sprev_ref[...].astype(BF16), wout_ref[...], (((1,), (1,)), ((), ())),
                         preferred_element_type=F32)
    d = d_ref[...]
    for r in range(tc):
        cols = slice(r * LANES, (r + 1) * LANES)
        o_ref[pl.ds(r, n_chunks, stride=tc), :] = _gelu_tanh(y[:, cols] + d * uv[:, cols])


def ssm_scan(u, ops, pw, layer, d_skip, batch):
    m_intra, w_in, w_out = ops
    pw_re, pw_im = pw
    t, width = u.shape
    lp = t // batch
    jt = width // LANES
    nc = lp // SSM_CHUNK
    kdim = SSM_CHUNK * LANES
    sw = pw_re.shape[-1]
    op = pl.BlockSpec((None, None, kdim, kdim), lambda j, b: (layer, j, 0, 0))
    pws = pl.BlockSpec((None, None, SSM_CHUNK + 1, sw), lambda j, b: (layer, j, 0, 0))
    return pl.pallas_call(
        functools.partial(_ssm_kernel, n_chunks=nc),
        grid=(jt, batch),
        in_specs=[pl.BlockSpec((lp, LANES), lambda j, b: (b, j)), op, op, op, pws, pws,
                  pl.BlockSpec((1, LANES), lambda j, b: (0, j))],
        out_specs=pl.BlockSpec((lp, LANES), lambda j, b: (b, j)),
        out_shape=jax.ShapeDtypeStruct((t, width), F32),
        scratch_shapes=[pltpu.VMEM((nc, 2 * sw), F32), pltpu.VMEM((nc, 2 * sw), F32)],
        compiler_params=_params(("parallel", "arbitrary")),
        name="ssm_scan",
    )(u, m_intra, w_in, w_out, pw_re, pw_im, d_skip.reshape(1, width))


def _glu_kernel(g_ref, w_ref, b_ref, gain_ref, o_ref):
    g = g_ref[...]
    zz = jnp.dot(g.astype(BF16), w_ref[...], preferred_element_type=F32) + b_ref[...]
    y = g * (1.0 / (1.0 + jnp.exp(-zz)))
    o_ref[...] = _rms(y, gain_ref[...]).astype(o_ref.dtype)


def glu_norm(g, w_stack, layer, bias, gain):
    t, width = g.shape
    tm = _pick_tile(t, 640, ROW_ALIGN)
    return pl.pallas_call(
        _glu_kernel,
        grid=(t // tm,),
        in_specs=[pl.BlockSpec((tm, width), lambda i: (i, 0)),
                  pl.BlockSpec((None, width, width), lambda i: (layer, 0, 0)),
                  pl.BlockSpec((1, width), lambda i: (0, 0)),
                  pl.BlockSpec((1, width), lambda i: (0, 0))],
        out_specs=pl.BlockSpec((tm, width), lambda i: (i, 0)),
        out_shape=jax.ShapeDtypeStruct((t, width), BF16),
        compiler_params=_params(("parallel",)),
        name="glu_norm",
    )(g, w_stack, bias.reshape(1, width), gain.reshape(1, width))


def _route_kernel(h_ref, gain_ref, wh_ref, wl_ref, bias_ref, hn_ref, eid_ref, gate_ref, *, n_groups, per_group):
    hn = _rms(h_ref[...], gain_ref[...])
    hn_ref[...] = hn
    xh, xl = _split_bf16(hn)
    wh = wh_ref[...]
    logits = (jnp.dot(xh, wh, preferred_element_type=F32) + jnp.dot(xl, wh, preferred_element_type=F32)
              + jnp.dot(xh, wl_ref[...], preferred_element_type=F32) + bias_ref[...])
    lane = lax.broadcasted_iota(jnp.int32, logits.shape, 1)

    def first_lane_of_max(vals, top):
        return jnp.min(jnp.where(vals == top, lane, LANES), axis=-1, keepdims=True)

    gmask = lane < n_groups
    lg = jnp.where(gmask, logits, MASKED)
    g_top = jnp.max(lg, axis=-1, keepdims=True)
    g_prob = 1.0 / jnp.sum(jnp.where(gmask, jnp.exp(lg - g_top), 0.0), axis=-1, keepdims=True)
    g_idx = first_lane_of_max(lg, g_top)
    lo_lane = n_groups + g_idx * per_group
    emask = jnp.logical_and(lane >= lo_lane, lane < lo_lane + per_group)
    le = jnp.where(emask, logits, MASKED)
    m1 = jnp.max(le, axis=-1, keepdims=True)
    i1 = first_lane_of_max(le, m1)
    denom = jnp.sum(jnp.where(emask, jnp.exp(le - m1), 0.0), axis=-1, keepdims=True)
    le2 = jnp.where(lane == i1, MASKED, le)
    m2 = jnp.max(le2, axis=-1, keepdims=True)
    i2 = first_lane_of_max(le2, m2)
    p1 = 1.0 / denom
    p2 = jnp.exp(m2 - m1) / denom
    psum = p1 + p2
    eid_ref[...] = jnp.where(lane == 0, i1 - n_groups, jnp.where(lane == 1, i2 - n_groups, 0))
    gate_ref[...] = jnp.where(lane == 0, g_prob * (p1 / psum), jnp.where(lane == 1, g_prob * (p2 / psum), 0.0))


def norm_route(h, gain, w_router, b_router, n_groups, per_group):
    t, d = h.shape
    tm = _pick_tile(t, 256, ROW_ALIGN)
    n_out = w_router.shape[-1]
    w_pad = jnp.zeros((d, LANES), F32).at[:, :n_out].set(w_router)
    wh, wl = _split_bf16(w_pad)
    bias = jnp.zeros((1, LANES), F32).at[0, :n_out].set(b_router)
    return pl.pallas_call(
        functools.partial(_route_kernel, n_groups=n_groups, per_group=per_group),
        grid=(t // tm,),
        in_specs=[pl.BlockSpec((tm, d), lambda i: (i, 0)),
                  pl.BlockSpec((1, d), lambda i: (0, 0)),
                  pl.BlockSpec((d, LANES), lambda i: (0, 0)),
                  pl.BlockSpec((d, LANES), lambda i: (0, 0)),
                  pl.BlockSpec((1, LANES), lambda i: (0, 0))],
        out_specs=[pl.BlockSpec((tm, d), lambda i: (i, 0)),
                   pl.BlockSpec((tm, LANES), lambda i: (i, 0)),
                   pl.BlockSpec((tm, LANES), lambda i: (i, 0))],
        out_shape=[jax.ShapeDtypeStruct((t, d), F32),
                   jax.ShapeDtypeStruct((t, LANES), jnp.int32),
                   jax.ShapeDtypeStruct((t, LANES), F32)],
        compiler_params=_params(("parallel",)),
        name="norm_route",
    )(h, gain.reshape(1, d), wh, wl, bias)


def dispatch_plan(eid, n_experts):
    n_tok, top_k = eid.shape
    n_assign = n_tok * top_k
    n_blocks = -(-n_assign // EXPERT_ROWS) + n_experts
    expert = eid.reshape(-1)
    iota = jnp.arange(n_assign, dtype=jnp.int32)
    experts = jnp.arange(n_experts, dtype=jnp.int32)
    expert_sorted, order = lax.sort((expert, iota), num_keys=1, is_stable=True)
    counts = jnp.sum((expert[:, None] == experts[None, :]).astype(jnp.int32), axis=0)
    padded = (counts + EXPERT_ROWS - 1) // EXPERT_ROWS * EXPERT_ROWS
    pad_end = jnp.cumsum(padded)
    pad_start = pad_end - padded
    start = jnp.cumsum(counts) - counts
    shift = pad_start - start
    dshift = shift - jnp.concatenate([jnp.zeros((1,), jnp.int32), shift[:-1]])
    dest = iota + jnp.sum(jnp.where(expert_sorted[:, None] >= experts[None, :], dshift[None, :], 0), axis=1)
    _, pos = lax.sort((order, dest), num_keys=1)
    blk_row0 = jnp.arange(n_blocks, dtype=jnp.int32) * EXPERT_ROWS
    block_expert = jnp.minimum(
        jnp.sum((pad_end[None, :] <= blk_row0[:, None]).astype(jnp.int32), axis=1), n_experts - 1)
    block_shift = jnp.sum(jnp.where(block_expert[:, None] == experts[None, :], shift[None, :], 0), axis=1)
    block_src = blk_row0 - block_shift
    n_used = (pad_end[-1] // EXPERT_ROWS).astype(jnp.int32).reshape(1)
    pos = pos.reshape(n_tok // COMBINE_ROWS, COMBINE_ROWS, top_k).transpose(0, 2, 1).reshape(-1)
    return order // top_k, block_expert, block_src, n_used, pos


def _gather_rows(src_hbm, dst_ref, sem, idx_ref, base, n_rows, limit=None, inline=False):
    def start(r):
        at = base + r
        if limit is not None:
            at = jnp.minimum(at, limit)
        row = idx_ref[at]
        pltpu.make_async_copy(src_hbm.at[pl.ds(row, 1), :], dst_ref.at[pl.ds(r, 1), :], sem).start()

    if inline:
        for r in range(n_rows):
            start(r)
    else:
        def body(r, c):
            start(r)
            return c
        lax.fori_loop(0, n_rows, body, 0, unroll=GATHER_UNROLL)


def _wait_rows(src_hbm, dst_ref, sem, n_rows):
    pltpu.make_async_copy(src_hbm.at[pl.ds(0, n_rows), :], dst_ref, sem).wait()


def _moe_kernel(be_ref, bs_ref, nu_ref, tok_ref, x_hbm, wg_ref, wu_ref, wd_ref, o_ref, xbuf0, xbuf1, sem):
    i = pl.program_id(0)
    n_used = nu_ref[0]
    last = tok_ref.shape[0] - 1
    bufs = (xbuf0, xbuf1)

    def fetch(blk, s, inline):
        _gather_rows(x_hbm, bufs[s], sem.at[s], tok_ref, bs_ref[blk], EXPERT_ROWS, limit=last, inline=inline)

    @pl.when(jnp.logical_and(i == 0, n_used > 0))
    def _():
        fetch(0, 0, False)

    def step(cur, nxt):
        _wait_rows(x_hbm, bufs[cur], sem.at[cur], EXPERT_ROWS)
        fetch(jnp.minimum(i + 1, n_used - 1), nxt, True)
        x = bufs[cur][...].astype(BF16)
        hg = jnp.dot(x, wg_ref[...], preferred_element_type=F32)
        hu = jnp.dot(x, wu_ref[...], preferred_element_type=F32)
        hid = hg * (1.0 / (1.0 + jnp.exp(-hg))) * hu
        o_ref[...] = jnp.dot(hid.astype(BF16), wd_ref[...], preferred_element_type=F32)

        @pl.when(i + 1 >= n_used)
        def _():
            _wait_rows(x_hbm, bufs[nxt], sem.at[nxt], EXPERT_ROWS)

    for parity in range(2):
        @pl.when(jnp.logical_and(i < n_used, lax.rem(i, 2) == parity))
        def _(parity=parity):
            step(parity, 1 - parity)

    @pl.when(i >= n_used)
    def _():
        o_ref[...] = jnp.zeros_like(o_ref)


def expert_blocks(hn, plan, w_gate, w_up, w_down, layer):
    tok_sorted, block_expert, block_src, n_used, _ = plan
    d = hn.shape[1]
    f = w_gate.shape[-1]
    n_blocks = block_expert.shape[0]
    n_rows = n_blocks * EXPERT_ROWS

    def wmap(i, be, bs, nu, tok):
        return (layer, be[i], 0, 0)

    return pl.pallas_call(
        _moe_kernel,
        grid_spec=pltpu.PrefetchScalarGridSpec(
            num_scalar_prefetch=4,
            grid=(n_blocks,),
            in_specs=[pl.BlockSpec(memory_space=pl.ANY),
                      pl.BlockSpec((None, None, d, f), wmap),
                      pl.BlockSpec((None, None, d, f), wmap),
                      pl.BlockSpec((None, None, f, d), wmap)],
            out_specs=pl.BlockSpec((EXPERT_ROWS, d), lambda i, be, bs, nu, tok: (i, 0)),
            scratch_shapes=[pltpu.VMEM((EXPERT_ROWS, d), F32), pltpu.VMEM((EXPERT_ROWS, d), F32),
                            pltpu.SemaphoreType.DMA((2,))]),
        out_shape=jax.ShapeDtypeStruct((n_rows, d), F32),
        compiler_params=_params(("arbitrary",)),
        name="expert_blocks",
    )(block_expert, block_src, n_used, tok_sorted, hn, w_gate, w_up, w_down)


def _combine_kernel(pos_ref, y_hbm, h_ref, gate_ref, gain_ref, oh_ref, on_ref, ybuf0, ybuf1, sem, *, top_k):
    i = pl.program_id(0)
    n = pl.num_programs(0)
    rows = COMBINE_ROWS * top_k
    bufs = (ybuf0, ybuf1)

    def fetch(blk, s, inline):
        _gather_rows(y_hbm, bufs[s], sem.at[s], pos_ref, blk * rows, rows, inline=inline)

    @pl.when(i == 0)
    def _():
        fetch(0, 0, False)

    def step(cur, nxt):
        _wait_rows(y_hbm, bufs[cur], sem.at[cur], rows)
        fetch(jnp.minimum(i + 1, n - 1), nxt, True)
        h_new = h_ref[...]
        for k in range(top_k):
            h_new = h_new + gate_ref[:, k:k + 1] * bufs[cur][k * COMBINE_ROWS:(k + 1) * COMBINE_ROWS, :]
        oh_ref[...] = h_new
        on_ref[...] = _rms(h_new, gain_ref[...]).astype(on_ref.dtype)

        @pl.when(i == n - 1)
        def _():
            _wait_rows(y_hbm, bufs[nxt], sem.at[nxt], rows)

    for parity in range(2):
        @pl.when(lax.rem(i, 2) == parity)
        def _(parity=parity):
            step(parity, 1 - parity)


def combine(h, y_rows, pos, gate, gain, norm_dtype, top_k):
    t, d = h.shape
    return pl.pallas_call(
        functools.partial(_combine_kernel, top_k=top_k),
        grid_spec=pltpu.PrefetchScalarGridSpec(
            num_scalar_prefetch=1,
            grid=(t // COMBINE_ROWS,),
            in_specs=[pl.BlockSpec(memory_space=pl.ANY),
                      pl.BlockSpec((COMBINE_ROWS, d), lambda i, pos: (i, 0)),
                      pl.BlockSpec((COMBINE_ROWS, LANES), lambda i, pos: (i, 0)),
                      pl.BlockSpec((1, d), lambda i, pos: (0, 0))],
            out_specs=[pl.BlockSpec((COMBINE_ROWS, d), lambda i, pos: (i, 0)),
                       pl.BlockSpec((COMBINE_ROWS, d), lambda i, pos: (i, 0))],
            scratch_shapes=[pltpu.VMEM((COMBINE_ROWS * top_k, d), F32), pltpu.VMEM((COMBINE_ROWS * top_k, d), F32),
                            pltpu.SemaphoreType.DMA((2,))]),
        out_shape=[jax.ShapeDtypeStruct((t, d), F32), jax.ShapeDtypeStruct((t, d), norm_dtype)],
        compiler_params=_params(("arbitrary",)),
        name="combine",
    )(pos, y_rows, h, gate, gain.reshape(1, d))


def kernel(x, meta_tokens, ln_mix, w_in, ssm_lam_re, ssm_lam_im, ssm_log_step, ssm_b_re, ssm_b_im, ssm_c_re, ssm_c_im, ssm_d, w_glu, b_glu, g_attn_out, g_ssm_out, w_out, ln_ffn, w_route_group, b_route_group, w_route_expert, b_route_expert, w_gate, w_up, w_down, ln_final):
    batch, seq, d_model = x.shape
    depth = w_in.shape[0]
    n_meta = meta_tokens.shape[0]
    ssm_width = ssm_d.shape[-1]
    attn_width = g_attn_out.shape[-1]
    n_heads = attn_width // HEAD_DIM
    n_groups = w_route_group.shape[-1]
    n_experts = w_route_expert.shape[-1]
    top_k = 2
    length = n_meta + seq
    lp = -(-length // ROW_ALIGN) * ROW_ALIGN
    t = batch * lp
    assert attn_width == ssm_width

    meta = jnp.broadcast_to(meta_tokens.astype(x.dtype)[None], (batch, n_meta, d_model))
    h = jnp.concatenate([meta, x, jnp.zeros((batch, lp - length, d_model), x.dtype)], axis=1).reshape(t, d_model)

    w_in_b = w_in.astype(BF16)
    w_glu_b = w_glu.astype(BF16)
    w_out_b = w_out.astype(BF16)
    w_gate_b = w_gate.astype(BF16)
    w_up_b = w_up.astype(BF16)
    w_down_b = w_down.astype(BF16)
    factors = _ssm_factors(ssm_lam_re, ssm_lam_im, ssm_log_step, ssm_b_re, ssm_b_im, ssm_c_re, ssm_c_im)
    ssm_ops = ssm_operators(factors)

    hn = rmsnorm(h, ln_mix[0], BF16)
    for l in range(depth):
        qkv = in_projection(hn, w_in_b, l, 0, 3 * attn_width, BF16, scaled_cols=attn_width,
                            scale=math.log2(math.e) * HEAD_DIM ** -0.5)
        u = in_projection(hn, w_in_b, l, 3 * attn_width, ssm_width, F32)
        attn_n = attention(qkv, g_attn_out[l], batch, n_heads)
        g = ssm_scan(u, ssm_ops, factors[4:], l, ssm_d[l], batch)
        ssm_n = glu_norm(g, w_glu_b, l, b_glu[l], g_ssm_out[l])
        h = out_projection(attn_n, ssm_n, w_out_b, l, h)
        w_router = jnp.concatenate([w_route_group[l], w_route_expert[l]], axis=-1)
        b_router = jnp.concatenate([b_route_group[l], b_route_expert[l]], axis=-1)
        hn2, eid, gate = norm_route(h, ln_ffn[l], w_router, b_router, n_groups, n_experts // n_groups)
        plan = dispatch_plan(eid[:, :top_k], n_experts)
        y_rows = expert_blocks(hn2, plan, w_gate_b, w_up_b, w_down_b, l)
        last = l == depth - 1
        next_gain = ln_final if last else ln_mix[l + 1]
        h, hn = combine(h, y_rows, plan[4], gate, next_gain, F32 if last else BF16, top_k)
    return hn.reshape(batch, lp, d_model)[:, n_meta:length]
```

```python
import functools
import math

import jax
import jax.numpy as jnp
from jax import lax
from jax.experimental import pallas as pl
from jax.experimental.pallas import tpu as pltpu

F32 = jnp.float32
BF16 = jnp.bfloat16

LANES = 128
HEAD_DIM = 128
Q_TILE = 128
ROW_ALIGN = 128
ATTN_WINDOW = 3
SSM_CHUNK = 8
EXPERT_ROWS = 256
COMBINE_ROWS = 128
GATHER_UNROLL = 8
NORM_EPS = 1e-6
EXP_UNDERFLOW = -151.0
MASKED = -1e30
VMEM_LIMIT = 56 * 1024 * 1024


def _pick_tile(n, target, align):
    best = align
    for t in range(align, min(n, target) + 1, align):
        if n % t == 0:
            best = t
    return best


def _params(sem, vmem=VMEM_LIMIT):
    return pltpu.CompilerParams(dimension_semantics=sem, vmem_limit_bytes=vmem)


def _rms(x, gain):
    return x * lax.rsqrt(jnp.mean(x * x, axis=-1, keepdims=True) + NORM_EPS) * gain


def _split_bf16(x):
    hi = x.astype(BF16)
    return hi, (x - hi.astype(F32)).astype(BF16)


def _norm_kernel(x_ref, g_ref, o_ref):
    o_ref[...] = _rms(x_ref[...], g_ref[...]).astype(o_ref.dtype)


def rmsnorm(x, gain, out_dtype):
    t, d = x.shape
    tm = _pick_tile(t, 512, ROW_ALIGN)
    return pl.pallas_call(
        _norm_kernel,
        grid=(t // tm,),
        in_specs=[pl.BlockSpec((tm, d), lambda i: (i, 0)),
                  pl.BlockSpec((1, d), lambda i: (0, 0))],
        out_specs=pl.BlockSpec((tm, d), lambda i: (i, 0)),
        out_shape=jax.ShapeDtypeStruct((t, d), out_dtype),
        compiler_params=_params(("parallel",)),
        name="rmsnorm",
    )(x, gain.reshape(1, d))


def _mm_kernel(a_ref, w_ref, o_ref, *, scaled_tiles, scale):
    acc = jnp.dot(a_ref[...], w_ref[...], preferred_element_type=F32)
    if scaled_tiles:
        acc = acc * jnp.where(pl.program_id(1) < scaled_tiles, scale, 1.0)
    o_ref[...] = acc.astype(o_ref.dtype)


def in_projection(a, w_stack, layer, col0, ncols, out_dtype, scaled_cols=0, scale=1.0):
    t, k = a.shape
    tm = _pick_tile(t, 1280, ROW_ALIGN)
    tn = _pick_tile(math.gcd(ncols, scaled_cols), 512, LANES)
    assert col0 % tn == 0
    return pl.pallas_call(
        functools.partial(_mm_kernel, scaled_tiles=scaled_cols // tn, scale=scale),
        grid=(t // tm, ncols // tn),
        in_specs=[pl.BlockSpec((tm, k), lambda i, j: (i, 0)),
                  pl.BlockSpec((None, k, tn), lambda i, j: (layer, 0, col0 // tn + j))],
        out_specs=pl.BlockSpec((tm, tn), lambda i, j: (i, j)),
        out_shape=jax.ShapeDtypeStruct((t, ncols), out_dtype),
        compiler_params=_params(("parallel", "arbitrary")),
        name="in_projection",
    )(a, w_stack)


def _out_proj_kernel(a1_ref, a2_ref, w1_ref, w2_ref, r_ref, o_ref):
    acc = jnp.dot(a1_ref[...], w1_ref[...], preferred_element_type=F32)
    acc += jnp.dot(a2_ref[...], w2_ref[...], preferred_element_type=F32)
    o_ref[...] = r_ref[...] + acc


def out_projection(a1, a2, w_stack, layer, resid):
    t, kh = a1.shape
    n = w_stack.shape[-1]
    tm = _pick_tile(t, 1280, ROW_ALIGN)
    tn = _pick_tile(n, 512, LANES)
    return pl.pallas_call(
        _out_proj_kernel,
        grid=(t // tm, n // tn),
        in_specs=[pl.BlockSpec((tm, kh), lambda i, j: (i, 0)),
                  pl.BlockSpec((tm, kh), lambda i, j: (i, 0)),
                  pl.BlockSpec((None, kh, tn), lambda i, j: (layer, 0, j)),
                  pl.BlockSpec((None, kh, tn), lambda i, j: (layer, 1, j)),
                  pl.BlockSpec((tm, tn), lambda i, j: (i, j))],
        out_specs=pl.BlockSpec((tm, tn), lambda i, j: (i, j)),
        out_shape=jax.ShapeDtypeStruct((t, n), F32),
        compiler_params=_params(("parallel", "arbitrary")),
        name="out_projection",
    )(a1, a2, w_stack, w_stack, resid)


def _log2_leave(z2):
    return jnp.maximum(z2, 0.0) + jnp.log2(1.0 + jnp.exp2(-jnp.abs(z2)))


def _attn_chunk(q_ref, k_ref, v_ref, tri_ref, carry_ref, acc_ref, h):
    hs = slice(h * HEAD_DIM, (h + 1) * HEAD_DIM)
    z2 = lax.dot_general(q_ref[:, hs], k_ref[:, hs], (((1,), (1,)), ((), ())), preferred_element_type=F32)
    hi, lo = _split_bf16(_log2_leave(z2))
    sums = jnp.dot(jnp.concatenate([hi, lo], axis=1), tri_ref[...], preferred_element_type=F32)
    carry = carry_ref[h]
    w = jnp.exp2(z2 + sums[:, :Q_TILE] + carry).astype(BF16)
    acc_ref[:, hs] += jnp.dot(w, v_ref[:, hs], preferred_element_type=F32)
    carry_ref[h] = carry + sums[:, Q_TILE:]


def _attn_window(q_ref, k_refs, v_refs, tri_ref, carry_ref, acc_ref, z_ref, hl_ref, i, n_heads):
    shape = (Q_TILE, Q_TILE)
    visible = lax.broadcasted_iota(jnp.int32, shape, 1) < lax.broadcasted_iota(jnp.int32, shape, 0)
    nt = (((1,), (1,)), ((), ()))
    for c in range(ATTN_WINDOW):
        for h in range(n_heads):
            hs = slice(h * HEAD_DIM, (h + 1) * HEAD_DIM)
            z2 = lax.dot_general(q_ref[:, hs], k_refs[c][:, hs], nt, preferred_element_type=F32)
            z_ref[c, h] = z2
            leave = _log2_leave(z2)
            if c == 0:
                leave = jnp.where(visible, leave, 0.0)
            hi, lo = _split_bf16(leave)
            hl_ref[c, h, :, :Q_TILE] = hi
            hl_ref[c, h, :, Q_TILE:] = lo
    for c in range(ATTN_WINDOW):
        tri = tri_ref[...]
        if c > 0:
            present = i >= c
            tri = jnp.where(present, tri, jnp.zeros_like(tri))
        for h in range(n_heads):
            hs = slice(h * HEAD_DIM, (h + 1) * HEAD_DIM)
            sums = jnp.dot(hl_ref[c, h], tri, preferred_element_type=F32)
            expo = z_ref[c, h] + sums[:, :Q_TILE]
            v = v_refs[c][:, hs]
            if c == 0:
                expo = jnp.where(visible, expo, MASKED)
            else:
                expo = jnp.minimum(expo + carry_ref[h], 0.0)
                v = jnp.where(present, v, jnp.zeros_like(v))
            pv = jnp.dot(jnp.exp2(expo).astype(BF16), v, preferred_element_type=F32)
            if c == 0:
                acc_ref[:, hs] = pv
                carry_ref[h] = sums[:, Q_TILE:]
            else:
                acc_ref[:, hs] += pv
                carry_ref[h] += sums[:, Q_TILE:]


def _attn_kernel(q_ref, k0_ref, k1_ref, k2_ref, v0_ref, v1_ref, v2_ref, kv_hbm, tri_ref, g_ref,
                 o_ref, carry_ref, acc_ref, z_ref, hl_ref, kt_ref, vt_ref, flag_ref, sem, *, n_heads, n_qblocks):
    b = pl.program_id(0)
    i = pl.program_id(1)
    width = n_heads * HEAD_DIM
    k_refs = (k0_ref, k1_ref, k2_ref)
    v_refs = (v0_ref, v1_ref, v2_ref)
    kb0 = i - ATTN_WINDOW

    def tail_copies(kb, slot):
        row0 = pl.multiple_of((b * n_qblocks + kb) * Q_TILE, Q_TILE)
        rows = pl.ds(row0, Q_TILE)
        return (pltpu.make_async_copy(kv_hbm.at[rows, pl.ds(width, width)], kt_ref.at[slot], sem.at[0, slot]),
                pltpu.make_async_copy(kv_hbm.at[rows, pl.ds(2 * width, width)], vt_ref.at[slot], sem.at[1, slot]))

    @pl.when(kb0 >= 0)
    def _():
        for cp in tail_copies(kb0, 0):
            cp.start()

    _attn_window(q_ref, k_refs, v_refs, tri_ref, carry_ref, acc_ref, z_ref, hl_ref, i, n_heads)

    def head_flags():
        n = jnp.int32(0)
        for h in range(n_heads):
            f = (jnp.max(carry_ref[h]) > EXP_UNDERFLOW).astype(jnp.int32)
            flag_ref[h] = f
            n = n + f
        return n

    m = carry_ref[0]
    for h in range(1, n_heads):
        m = jnp.maximum(m, carry_ref[h])
    any_left = jnp.logical_and(jnp.max(m) > EXP_UNDERFLOW, kb0 >= 0)

    @pl.when(any_left)
    def _():
        head_flags()

    def tail_cond(state):
        kb, n_active = state
        return jnp.logical_and(kb >= 0, n_active > 0)

    def tail_body(state):
        kb, _ = state
        slot = lax.rem(kb0 - kb, 2)
        for cp in tail_copies(kb, slot):
            cp.wait()

        @pl.when(kb >= 1)
        def _():
            for cp in tail_copies(kb - 1, 1 - slot):
                cp.start()

        for h in range(n_heads):
            @pl.when(flag_ref[h] > 0)
            def _(h=h):
                _attn_chunk(q_ref, kt_ref.at[slot], vt_ref.at[slot], tri_ref, carry_ref, acc_ref, h)
        return kb - 1, head_flags()

    kb_end, _ = lax.while_loop(tail_cond, tail_body, (kb0, any_left.astype(jnp.int32)))

    @pl.when(kb_end >= 0)
    def _():
        for cp in tail_copies(kb_end, lax.rem(kb0 - kb_end, 2)):
            cp.wait()

    o_ref[...] = _rms(acc_ref[...], g_ref[...]).astype(o_ref.dtype)


def attention(qkv, gain, batch, n_heads):
    t = qkv.shape[0]
    width = n_heads * HEAD_DIM
    nq = t // batch // Q_TILE
    j = jnp.arange(2 * Q_TILE)[:, None] % Q_TILE
    s = jnp.arange(2 * Q_TILE)[None, :]
    tri = -jnp.where(s < Q_TILE, j >= s, True).astype(BF16)

    def qmap(b, i):
        return (b * nq + i, 0)

    def kvmap(c, col):
        return lambda b, i: (b * nq + jnp.maximum(i - c, 0), col)

    blk = (Q_TILE, width)
    return pl.pallas_call(
        functools.partial(_attn_kernel, n_heads=n_heads, n_qblocks=nq),
        grid=(batch, nq),
        in_specs=[pl.BlockSpec(blk, qmap)]
        + [pl.BlockSpec(blk, kvmap(c, 1)) for c in range(ATTN_WINDOW)]
        + [pl.BlockSpec(blk, kvmap(c, 2)) for c in range(ATTN_WINDOW)]
        + [pl.BlockSpec(memory_space=pl.ANY),
           pl.BlockSpec((2 * Q_TILE, 2 * Q_TILE), lambda b, i: (0, 0)),
           pl.BlockSpec((1, width), lambda b, i: (0, 0))],
        out_specs=pl.BlockSpec(blk, qmap),
        out_shape=jax.ShapeDtypeStruct((t, width), BF16),
        scratch_shapes=[pltpu.VMEM((n_heads, Q_TILE, Q_TILE), F32),
                        pltpu.VMEM((Q_TILE, width), F32),
                        pltpu.VMEM((ATTN_WINDOW, n_heads, Q_TILE, Q_TILE), F32),
                        pltpu.VMEM((ATTN_WINDOW, n_heads, Q_TILE, 2 * Q_TILE), BF16),
                        pltpu.VMEM((2, Q_TILE, width), BF16),
                        pltpu.VMEM((2, Q_TILE, width), BF16),
                        pltpu.SMEM((n_heads,), jnp.int32),
                        pltpu.SemaphoreType.DMA((2, 2))],
        compiler_params=_params(("parallel", "arbitrary")),
        name="stickbreak_attention",
    )(qkv, qkv, qkv, qkv, qkv, qkv, qkv, qkv, tri, gain.reshape(1, width))


def _ssm_factors(lam_re, lam_im, log_step, b_re, b_im, c_re, c_im):
    depth, g, p, c = b_re.shape
    gp = LANES // c
    jt = g // gp
    lr = lam_re.astype(F32)[:, None]
    li = lam_im.astype(F32)[:, None]
    step = jnp.exp(log_step.astype(F32))[:, None, :, None]
    tau = jnp.arange(SSM_CHUNK + 1, dtype=F32)[None, :, None, None]
    mag = jnp.exp(lr * step * tau)
    ang = li * step * tau
    pw_re = mag * jnp.cos(ang)
    pw_im = mag * jnp.sin(ang)
    abar_re, abar_im = pw_re[:, 1], pw_im[:, 1]
    lr, li = lr[:, 0], li[:, 0]
    den = lr * lr + li * li
    num_re = abar_re - 1.0
    coef_re = ((num_re * lr + abar_im * li) / den)[..., None]
    coef_im = ((abar_im * lr - num_re * li) / den)[..., None]
    br = b_re.astype(F32)
    bi = b_im.astype(F32)
    bbar_re = coef_re * br - coef_im * bi
    bbar_im = coef_re * bi + coef_im * br
    eye = jnp.eye(gp, dtype=F32)

    def expand_b(x):
        x = x.reshape(depth, jt, gp, p, c).transpose(0, 1, 2, 4, 3)
        x = x[:, :, :, :, None, :] * eye[None, None, :, None, :, None]
        return x.reshape(depth, jt, gp * c, gp * p)

    def expand_c(x):
        x = x.astype(F32).reshape(depth, jt, gp, c, p)
        x = x[:, :, :, :, None, :] * eye[None, None, :, None, :, None]
        return x.reshape(depth, jt, gp * c, gp * p)

    def lane_tiles(x):
        return x.reshape(depth, SSM_CHUNK + 1, jt, gp * p).transpose(0, 2, 1, 3)

    return (expand_b(bbar_re), expand_b(bbar_im), expand_c(c_re), expand_c(c_im),
            lane_tiles(pw_re), lane_tiles(pw_im))


def _ssm_build_kernel(bbr_ref, bbi_ref, ctr_ref, cti_ref, pwr_ref, pwi_ref, m_ref, win_ref, wout_ref):
    tc = SSM_CHUNK
    bb_re = bbr_ref[...]
    bb_im = bbi_ref[...]
    ct_re = ctr_ref[...]
    ct_im = cti_ref[...]
    half = bb_re.shape[1]
    nt = (((1,), (1,)), ((), ()))

    def dot_nt(a, b):
        ah, al = _split_bf16(a)
        bh, bl = _split_bf16(b)
        return (lax.dot_general(ah, bh, nt, preferred_element_type=F32)
                + lax.dot_general(al, bh, nt, preferred_element_type=F32)
                + lax.dot_general(ah, bl, nt, preferred_element_type=F32))

    def rows(n):
        return slice(n * LANES, (n + 1) * LANES)

    zero = jnp.zeros((LANES, LANES), BF16)
    for s in range(tc):
        for r in range(s):
            m_ref[rows(s), rows(r)] = zero
    for tau in range(tc + 1):
        pr = pwr_ref[tau:tau + 1, :]
        pi = pwi_ref[tau:tau + 1, :]
        cp_re = ct_re * pr - ct_im * pi
        cp_im = ct_re * pi + ct_im * pr
        if tau < tc:
            lag_block = (dot_nt(bb_re, cp_re) - dot_nt(bb_im, cp_im)).astype(BF16)
            for s in range(tc - tau):
                m_ref[rows(s), rows(s + tau)] = lag_block
            s = tc - 1 - tau
            win_ref[rows(s), :half] = (bb_re * pr - bb_im * pi).astype(BF16)
            win_ref[rows(s), half:] = (bb_re * pi + bb_im * pr).astype(BF16)
        if tau >= 1:
            wout_ref[rows(tau - 1), :half] = cp_re.astype(BF16)
            wout_ref[rows(tau - 1), half:] = (-cp_im).astype(BF16)


def ssm_operators(factors):
    bb_re, bb_im, ct_re, ct_im, pw_re, pw_im = factors
    depth, jt, cw, sw = bb_re.shape
    kdim = SSM_CHUNK * LANES
    fac = pl.BlockSpec((None, None, cw, sw), lambda d, j: (d, j, 0, 0))
    pws = pl.BlockSpec((None, None, SSM_CHUNK + 1, sw), lambda d, j: (d, j, 0, 0))
    out = pl.BlockSpec((None, None, kdim, kdim), lambda d, j: (d, j, 0, 0))
    assert 2 * sw == kdim
    shape = jax.ShapeDtypeStruct((depth, jt, kdim, kdim), BF16)
    return pl.pallas_call(
        _ssm_build_kernel,
        grid=(depth, jt),
        in_specs=[fac, fac, fac, fac, pws, pws],
        out_specs=[out, out, out],
        out_shape=[shape, shape, shape],
        compiler_params=_params(("parallel", "parallel")),
        name="ssm_operators",
    )(bb_re, bb_im, ct_re, ct_im, pw_re, pw_im)


def _gelu_tanh(x):
    return 0.5 * x * (1.0 + jnp.tanh(math.sqrt(2.0 / math.pi) * (x + 0.044715 * (x * x * x))))


def _ssm_kernel(u_ref, m_ref, win_ref, wout_ref, pwr_ref, pwi_ref, d_ref, o_ref, e_ref, sprev_ref, *, n_chunks):
    tc = SSM_CHUNK
    half = pwr_ref.shape[-1]
    uv = jnp.concatenate([u_ref[pl.ds(r, n_chunks, stride=tc), :] for r in range(tc)], axis=1)
    uvb = uv.astype(BF16)
    e_ref[...] = jnp.dot(uvb, win_ref[...], preferred_element_type=F32)
    a_re = pwr_ref[tc:tc + 1, :]
    a_im = pwi_ref[tc:tc + 1, :]

    def step(n, state):
        s_re, s_im = state
        sprev_ref[pl.ds(n, 1), :half] = s_re
        sprev_ref[pl.ds(n, 1), half:] = s_im
        e = e_ref[pl.ds(n, 1), :]
        return (a_re * s_re - a_im * s_im + e[:, :half],
                a_re * s_im + a_im * s_re + e[:, half:])

    zero = jnp.zeros((1, half), F32)
    lax.fori_loop(0, n_chunks, step, (zero, zero))
    y = jnp.dot(uvb, m_ref[...], preferred_element_type=F32)
    y += lax.dot_general(sprev_ref[...].astype(BF16), wout_ref[...], (((1,), (1,)), ((), ())),
                         preferred_element_type=F32)
    d = d_ref[...]
    for r in range(tc):
        cols = slice(r * LANES, (r + 1) * LANES)
        o_ref[pl.ds(r, n_chunks, stride=tc), :] = _gelu_tanh(y[:, cols] + d * uv[:, cols])


def ssm_scan(u, ops, pw, layer, d_skip, batch):
    m_intra, w_in, w_out = ops
    pw_re, pw_im = pw
    t, width = u.shape
    lp = t // batch
    jt = width // LANES
    nc = lp // SSM_CHUNK
    kdim = SSM_CHUNK * LANES
    sw = pw_re.shape[-1]
    op = pl.BlockSpec((None, None, kdim, kdim), lambda j, b: (layer, j, 0, 0))
    pws = pl.BlockSpec((None, None, SSM_CHUNK + 1, sw), lambda j, b: (layer, j, 0, 0))
    return pl.pallas_call(
        functools.partial(_ssm_kernel, n_chunks=nc),
        grid=(jt, batch),
        in_specs=[pl.BlockSpec((lp, LANES), lambda j, b: (b, j)), op, op, op, pws, pws,
                  pl.BlockSpec((1, LANES), lambda j, b: (0, j))],
        out_specs=pl.BlockSpec((lp, LANES), lambda j, b: (b, j)),
        out_shape=jax.ShapeDtypeStruct((t, width), F32),
        scratch_shapes=[pltpu.VMEM((nc, 2 * sw), F32), pltpu.VMEM((nc, 2 * sw), F32)],
        compiler_params=_params(("parallel", "arbitrary")),
        name="ssm_scan",
    )(u, m_intra, w_in, w_out, pw_re, pw_im, d_skip.reshape(1, width))


def _glu_kernel(g_ref, w_ref, b_ref, gain_ref, o_ref):
    g = g_ref[...]
    zz = jnp.dot(g.astype(BF16), w_ref[...], preferred_element_type=F32) + b_ref[...]
    y = g * (1.0 / (1.0 + jnp.exp(-zz)))
    o_ref[...] = _rms(y, gain_ref[...]).astype(o_ref.dtype)


def glu_norm(g, w_stack, layer, bias, gain):
    t, width = g.shape
    tm = _pick_tile(t, 640, ROW_ALIGN)
    return pl.pallas_call(
        _glu_kernel,
        grid=(t // tm,),
        in_specs=[pl.BlockSpec((tm, width), lambda i: (i, 0)),
                  pl.BlockSpec((None, width, width), lambda i: (layer, 0, 0)),
                  pl.BlockSpec((1, width), lambda i: (0, 0)),
                  pl.BlockSpec((1, width), lambda i: (0, 0))],
        out_specs=pl.BlockSpec((tm, width), lambda i: (i, 0)),
        out_shape=jax.ShapeDtypeStruct((t, width), BF16),
        compiler_params=_params(("parallel",)),
        name="glu_norm",
    )(g, w_stack, bias.reshape(1, width), gain.reshape(1, width))


def _slab_pitch(d):
    return d // LANES + 1


def _slab_store(ref, first, x):
    n, d = x.shape
    pitch = _slab_pitch(d)
    for c in range(d // LANES):
        ref[pl.ds(first * pitch + c, n, stride=pitch), :] = x[:, c * LANES:(c + 1) * LANES]
    ref[pl.ds(first * pitch + d // LANES, n, stride=pitch), :] = jnp.zeros((n, LANES), x.dtype)


def _slab_tile(ref, first, n, c, pitch):
    return ref[pl.ds(first * pitch + c, n, stride=pitch), :]


def _route_kernel(h_ref, gain_ref, wh_ref, wl_ref, bias_ref, hn_ref, eid_ref, gate_ref, *, n_groups, per_group):
    hn = _rms(h_ref[...], gain_ref[...])
    _slab_store(hn_ref, 0, hn)
    xh, xl = _split_bf16(hn)
    wh = wh_ref[...]
    logits = (jnp.dot(xh, wh, preferred_element_type=F32) + jnp.dot(xl, wh, preferred_element_type=F32)
              + jnp.dot(xh, wl_ref[...], preferred_element_type=F32) + bias_ref[...])
    lane = lax.broadcasted_iota(jnp.int32, logits.shape, 1)

    def first_lane_of_max(vals, top):
        return jnp.min(jnp.where(vals == top, lane, LANES), axis=-1, keepdims=True)

    gmask = lane < n_groups
    lg = jnp.where(gmask, logits, MASKED)
    g_top = jnp.max(lg, axis=-1, keepdims=True)
    g_prob = 1.0 / jnp.sum(jnp.where(gmask, jnp.exp(lg - g_top), 0.0), axis=-1, keepdims=True)
    g_idx = first_lane_of_max(lg, g_top)
    lo_lane = n_groups + g_idx * per_group
    emask = jnp.logical_and(lane >= lo_lane, lane < lo_lane + per_group)
    le = jnp.where(emask, logits, MASKED)
    m1 = jnp.max(le, axis=-1, keepdims=True)
    i1 = first_lane_of_max(le, m1)
    denom = jnp.sum(jnp.where(emask, jnp.exp(le - m1), 0.0), axis=-1, keepdims=True)
    le2 = jnp.where(lane == i1, MASKED, le)
    m2 = jnp.max(le2, axis=-1, keepdims=True)
    i2 = first_lane_of_max(le2, m2)
    p1 = 1.0 / denom
    p2 = jnp.exp(m2 - m1) / denom
    psum = p1 + p2
    eid_ref[...] = jnp.where(lane == 0, i1 - n_groups, jnp.where(lane == 1, i2 - n_groups, 0))
    gate_ref[...] = jnp.where(lane == 0, g_prob * (p1 / psum), jnp.where(lane == 1, g_prob * (p2 / psum), 0.0))


def norm_route(h, gain, w_router, b_router, n_groups, per_group):
    t, d = h.shape
    tm = _pick_tile(t, 256, ROW_ALIGN)
    pitch = _slab_pitch(d)
    n_out = w_router.shape[-1]
    w_pad = jnp.zeros((d, LANES), F32).at[:, :n_out].set(w_router)
    wh, wl = _split_bf16(w_pad)
    bias = jnp.zeros((1, LANES), F32).at[0, :n_out].set(b_router)
    return pl.pallas_call(
        functools.partial(_route_kernel, n_groups=n_groups, per_group=per_group),
        grid=(t // tm,),
        in_specs=[pl.BlockSpec((tm, d), lambda i: (i, 0)),
                  pl.BlockSpec((1, d), lambda i: (0, 0)),
                  pl.BlockSpec((d, LANES), lambda i: (0, 0)),
                  pl.BlockSpec((d, LANES), lambda i: (0, 0)),
                  pl.BlockSpec((1, LANES), lambda i: (0, 0))],
        out_specs=[pl.BlockSpec((tm * pitch, LANES), lambda i: (i, 0)),
                   pl.BlockSpec((tm, LANES), lambda i: (i, 0)),
                   pl.BlockSpec((tm, LANES), lambda i: (i, 0))],
        out_shape=[jax.ShapeDtypeStruct((t * pitch, LANES), F32),
                   jax.ShapeDtypeStruct((t, LANES), jnp.int32),
                   jax.ShapeDtypeStruct((t, LANES), F32)],
        compiler_params=_params(("parallel",)),
        name="norm_route",
    )(h, gain.reshape(1, d), wh, wl, bias)


def dispatch_plan(eid, n_experts):
    n_tok, top_k = eid.shape
    n_assign = n_tok * top_k
    n_blocks = -(-n_assign // EXPERT_ROWS) + n_experts
    expert = eid.reshape(-1)
    iota = jnp.arange(n_assign, dtype=jnp.int32)
    experts = jnp.arange(n_experts, dtype=jnp.int32)
    expert_sorted, order = lax.sort((expert, iota), num_keys=1, is_stable=True)
    counts = jnp.sum((expert[:, None] == experts[None, :]).astype(jnp.int32), axis=0)
    padded = (counts + EXPERT_ROWS - 1) // EXPERT_ROWS * EXPERT_ROWS
    pad_end = jnp.cumsum(padded)
    pad_start = pad_end - padded
    start = jnp.cumsum(counts) - counts
    shift = pad_start - start
    dshift = shift - jnp.concatenate([jnp.zeros((1,), jnp.int32), shift[:-1]])
    dest = iota + jnp.sum(jnp.where(expert_sorted[:, None] >= experts[None, :], dshift[None, :], 0), axis=1)
    _, pos = lax.sort((order, dest), num_keys=1)
    blk_row0 = jnp.arange(n_blocks, dtype=jnp.int32) * EXPERT_ROWS
    block_expert = jnp.minimum(
        jnp.sum((pad_end[None, :] <= blk_row0[:, None]).astype(jnp.int32), axis=1), n_experts - 1)
    block_shift = jnp.sum(jnp.where(block_expert[:, None] == experts[None, :], shift[None, :], 0), axis=1)
    block_src = blk_row0 - block_shift
    n_used = (pad_end[-1] // EXPERT_ROWS).astype(jnp.int32).reshape(1)
    pos = pos.reshape(n_tok // COMBINE_ROWS, COMBINE_ROWS, top_k).transpose(0, 2, 1).reshape(-1)
    return order // top_k, block_expert, block_src, n_used, pos


def _gather_slabs(src_hbm, dst_ref, sem, idx_ref, base, n_slabs, d, limit=None, inline=False):
    pitch = _slab_pitch(d)
    tiles = d // LANES

    def start(r):
        at = base + r
        if limit is not None:
            at = jnp.minimum(at, limit)
        src = src_hbm.at[pl.ds(idx_ref[at] * pitch, tiles), :]
        pltpu.make_async_copy(src, dst_ref.at[pl.ds(r * pitch, tiles), :], sem).start()

    if inline:
        for r in range(n_slabs):
            start(r)
    else:
        def body(r, c):
            start(r)
            return c
        lax.fori_loop(0, n_slabs, body, 0, unroll=GATHER_UNROLL)


def _wait_slabs(src_hbm, dst_ref, sem, n_slabs, d):
    rows = n_slabs * (d // LANES)
    pltpu.make_async_copy(src_hbm.at[pl.ds(0, rows), :], dst_ref.at[pl.ds(0, rows), :], sem).wait()


def _moe_kernel(be_ref, bs_ref, nu_ref, tok_ref, x_hbm, wg_ref, wu_ref, wd_ref, o_ref, xbuf0, xbuf1, sem):
    i = pl.program_id(0)
    n_used = nu_ref[0]
    last = tok_ref.shape[0] - 1
    bufs = (xbuf0, xbuf1)
    d = wg_ref.shape[0]
    pitch = _slab_pitch(d)

    def fetch(blk, s, inline):
        _gather_slabs(x_hbm, bufs[s], sem.at[s], tok_ref, bs_ref[blk], EXPERT_ROWS, d, limit=last, inline=inline)

    @pl.when(jnp.logical_and(i == 0, n_used > 0))
    def _():
        fetch(0, 0, False)

    def step(cur, nxt):
        _wait_slabs(x_hbm, bufs[cur], sem.at[cur], EXPERT_ROWS, d)
        fetch(jnp.minimum(i + 1, n_used - 1), nxt, True)
        x = jnp.concatenate([_slab_tile(bufs[cur], 0, EXPERT_ROWS, c, pitch).astype(BF16)
                             for c in range(d // LANES)], axis=1)
        hg = jnp.dot(x, wg_ref[...], preferred_element_type=F32)
        hu = jnp.dot(x, wu_ref[...], preferred_element_type=F32)
        hid = hg * (1.0 / (1.0 + jnp.exp(-hg))) * hu
        _slab_store(o_ref, 0, jnp.dot(hid.astype(BF16), wd_ref[...], preferred_element_type=F32))

        @pl.when(i + 1 >= n_used)
        def _():
            _wait_slabs(x_hbm, bufs[nxt], sem.at[nxt], EXPERT_ROWS, d)

    for parity in range(2):
        @pl.when(jnp.logical_and(i < n_used, lax.rem(i, 2) == parity))
        def _(parity=parity):
            step(parity, 1 - parity)

    @pl.when(i >= n_used)
    def _():
        o_ref[...] = jnp.zeros_like(o_ref)


def expert_blocks(hn, plan, w_gate, w_up, w_down, layer):
    tok_sorted, block_expert, block_src, n_used, _ = plan
    d, f = w_gate.shape[-2:]
    n_blocks = block_expert.shape[0]
    n_rows = n_blocks * EXPERT_ROWS
    buf_rows = EXPERT_ROWS * _slab_pitch(d)

    def wmap(i, be, bs, nu, tok):
        return (layer, be[i], 0, 0)

    return pl.pallas_call(
        _moe_kernel,
        grid_spec=pltpu.PrefetchScalarGridSpec(
            num_scalar_prefetch=4,
            grid=(n_blocks,),
            in_specs=[pl.BlockSpec(memory_space=pl.ANY),
                      pl.BlockSpec((None, None, d, f), wmap),
                      pl.BlockSpec((None, None, d, f), wmap),
                      pl.BlockSpec((None, None, f, d), wmap)],
            out_specs=pl.BlockSpec((buf_rows, LANES), lambda i, be, bs, nu, tok: (i, 0)),
            scratch_shapes=[pltpu.VMEM((buf_rows, LANES), F32), pltpu.VMEM((buf_rows, LANES), F32),
                            pltpu.SemaphoreType.DMA((2,))]),
        out_shape=jax.ShapeDtypeStruct((n_blocks * buf_rows, LANES), F32),
        compiler_params=_params(("arbitrary",)),
        name="expert_blocks",
    )(block_expert, block_src, n_used, tok_sorted, hn, w_gate, w_up, w_down)


def _combine_kernel(pos_ref, y_hbm, h_ref, gate_ref, gain_ref, oh_ref, on_ref, ybuf0, ybuf1, sem, *, top_k):
    i = pl.program_id(0)
    n = pl.num_programs(0)
    rows = COMBINE_ROWS * top_k
    bufs = (ybuf0, ybuf1)
    d = h_ref.shape[-1]
    pitch = _slab_pitch(d)

    def fetch(blk, s, inline):
        _gather_slabs(y_hbm, bufs[s], sem.at[s], pos_ref, blk * rows, rows, d, inline=inline)

    @pl.when(i == 0)
    def _():
        fetch(0, 0, False)

    def step(cur, nxt):
        _wait_slabs(y_hbm, bufs[cur], sem.at[cur], rows, d)
        fetch(jnp.minimum(i + 1, n - 1), nxt, True)
        gates = [gate_ref[:, k:k + 1] for k in range(top_k)]
        tiles = []
        for c in range(d // LANES):
            tile = h_ref[:, c * LANES:(c + 1) * LANES]
            for k in range(top_k):
                tile = tile + gates[k] * _slab_tile(bufs[cur], k * COMBINE_ROWS, COMBINE_ROWS, c, pitch)
            tiles.append(tile)
        h_new = jnp.concatenate(tiles, axis=1)
        oh_ref[...] = h_new
        on_ref[...] = _rms(h_new, gain_ref[...]).astype(on_ref.dtype)

        @pl.when(i == n - 1)
        def _():
            _wait_slabs(y_hbm, bufs[nxt], sem.at[nxt], rows, d)

    for parity in range(2):
        @pl.when(lax.rem(i, 2) == parity)
        def _(parity=parity):
            step(parity, 1 - parity)


def combine(h, y_rows, pos, gate, gain, norm_dtype, top_k):
    t, d = h.shape
    buf_rows = COMBINE_ROWS * top_k * _slab_pitch(d)
    return pl.pallas_call(
        functools.partial(_combine_kernel, top_k=top_k),
        grid_spec=pltpu.PrefetchScalarGridSpec(
            num_scalar_prefetch=1,
            grid=(t // COMBINE_ROWS,),
            in_specs=[pl.BlockSpec(memory_space=pl.ANY),
                      pl.BlockSpec((COMBINE_ROWS, d), lambda i, pos: (i, 0)),
                      pl.BlockSpec((COMBINE_ROWS, LANES), lambda i, pos: (i, 0)),
                      pl.BlockSpec((1, d), lambda i, pos: (0, 0))],
            out_specs=[pl.BlockSpec((COMBINE_ROWS, d), lambda i, pos: (i, 0)),
                       pl.BlockSpec((COMBINE_ROWS, d), lambda i, pos: (i, 0))],
            scratch_shapes=[pltpu.VMEM((buf_rows, LANES), F32), pltpu.VMEM((buf_rows, LANES), F32),
                            pltpu.SemaphoreType.DMA((2,))]),
        out_shape=[jax.ShapeDtypeStruct((t, d), F32), jax.ShapeDtypeStruct((t, d), norm_dtype)],
        compiler_params=_params(("arbitrary",)),
        name="combine",
    )(pos, y_rows, h, gate, gain.reshape(1, d))


def kernel(x, meta_tokens, ln_mix, w_in, ssm_lam_re, ssm_lam_im, ssm_log_step, ssm_b_re, ssm_b_im, ssm_c_re, ssm_c_im, ssm_d, w_glu, b_glu, g_attn_out, g_ssm_out, w_out, ln_ffn, w_route_group, b_route_group, w_route_expert, b_route_expert, w_gate, w_up, w_down, ln_final):
    batch, seq, d_model = x.shape
    depth = w_in.shape[0]
    n_meta = meta_tokens.shape[0]
    ssm_width = ssm_d.shape[-1]
    attn_width = g_attn_out.shape[-1]
    n_heads = attn_width // HEAD_DIM
    n_groups = w_route_group.shape[-1]
    n_experts = w_route_expert.shape[-1]
    top_k = 2
    length = n_meta + seq
    lp = -(-length // ROW_ALIGN) * ROW_ALIGN
    t = batch * lp
    assert attn_width == ssm_width

    meta = jnp.broadcast_to(meta_tokens.astype(x.dtype)[None], (batch, n_meta, d_model))
    h = jnp.concatenate([meta, x, jnp.zeros((batch, lp - length, d_model), x.dtype)], axis=1).reshape(t, d_model)

    w_in_b = w_in.astype(BF16)
    w_glu_b = w_glu.astype(BF16)
    w_out_b = w_out.astype(BF16)
    w_gate_b = w_gate.astype(BF16)
    w_up_b = w_up.astype(BF16)
    w_down_b = w_down.astype(BF16)
    factors = _ssm_factors(ssm_lam_re, ssm_lam_im, ssm_log_step, ssm_b_re, ssm_b_im, ssm_c_re, ssm_c_im)
    ssm_ops = ssm_operators(factors)

    hn = rmsnorm(h, ln_mix[0], BF16)
    for l in range(depth):
        qkv = in_projection(hn, w_in_b, l, 0, 3 * attn_width, BF16, scaled_cols=attn_width,
                            scale=math.log2(math.e) * HEAD_DIM ** -0.5)
        u = in_projection(hn, w_in_b, l, 3 * attn_width, ssm_width, F32)
        attn_n = attention(qkv, g_attn_out[l], batch, n_heads)
        g = ssm_scan(u, ssm_ops, factors[4:], l, ssm_d[l], batch)
        ssm_n = glu_norm(g, w_glu_b, l, b_glu[l], g_ssm_out[l])
        h = out_projection(attn_n, ssm_n, w_out_b, l, h)
        w_router = jnp.concatenate([w_route_group[l], w_route_expert[l]], axis=-1)
        b_router = jnp.concatenate([b_route_group[l], b_route_expert[l]], axis=-1)
        hn2, eid, gate = norm_route(h, ln_ffn[l], w_router, b_router, n_groups, n_experts // n_groups)
        plan = dispatch_plan(eid[:, :top_k], n_experts)
        y_rows = expert_blocks(hn2, plan, w_gate_b, w_up_b, w_down_b, l)
        last = l == depth - 1
        next_gain = ln_final if last else ln_mix[l + 1]
        h, hn = combine(h, y_rows, plan[4], gate, next_gain, F32 if last else BF16, top_k)
    return hn.reshape(batch, lp, d_model)[:, n_meta:length]
```

```python
import functools
import math

import jax
import jax.numpy as jnp
from jax import lax
from jax.experimental import pallas as pl
from jax.experimental.pallas import tpu as pltpu

F32 = jnp.float32
BF16 = jnp.bfloat16

LANES = 128
HEAD_DIM = 128
Q_TILE = 128
ROW_ALIGN = 128
ATTN_WINDOW = 3
SSM_CHUNK = 8
EXPERT_ROWS = 256
COMBINE_ROWS = 128
GATHER_UNROLL = 8
NORM_EPS = 1e-6
EXP_UNDERFLOW = -151.0
MASKED = -1e30
VMEM_LIMIT = 56 * 1024 * 1024


def _pick_tile(n, target, align):
    best = align
    for t in range(align, min(n, target) + 1, align):
        if n % t == 0:
            best = t
    return best


def _params(sem, vmem=VMEM_LIMIT):
    return pltpu.CompilerParams(dimension_semantics=sem, vmem_limit_bytes=vmem)


def _rms(x, gain):
    return x * lax.rsqrt(jnp.mean(x * x, axis=-1, keepdims=True) + NORM_EPS) * gain


def _split_bf16(x):
    hi = x.astype(BF16)
    return hi, (x - hi.astype(F32)).astype(BF16)


def _norm_kernel(x_ref, g_ref, o_ref):
    o_ref[...] = _rms(x_ref[...], g_ref[...]).astype(o_ref.dtype)


def rmsnorm(x, gain, out_dtype):
    t, d = x.shape
    tm = _pick_tile(t, 512, ROW_ALIGN)
    return pl.pallas_call(
        _norm_kernel,
        grid=(t // tm,),
        in_specs=[pl.BlockSpec((tm, d), lambda i: (i, 0)),
                  pl.BlockSpec((1, d), lambda i: (0, 0))],
        out_specs=pl.BlockSpec((tm, d), lambda i: (i, 0)),
        out_shape=jax.ShapeDtypeStruct((t, d), out_dtype),
        compiler_params=_params(("parallel",)),
        name="rmsnorm",
    )(x, gain.reshape(1, d))


def _mm_kernel(a_ref, w_ref, o_ref, *, scaled_tiles, scale):
    acc = jnp.dot(a_ref[...], w_ref[...].astype(BF16), preferred_element_type=F32)
    if scaled_tiles:
        acc = acc * jnp.where(pl.program_id(1) < scaled_tiles, scale, 1.0)
    o_ref[...] = acc.astype(o_ref.dtype)


def in_projection(a, w_stack, layer, col0, ncols, out_dtype, scaled_cols=0, scale=1.0):
    t, k = a.shape
    tm = _pick_tile(t, 1280, ROW_ALIGN)
    tn = _pick_tile(math.gcd(ncols, scaled_cols), 512, LANES)
    assert col0 % tn == 0
    return pl.pallas_call(
        functools.partial(_mm_kernel, scaled_tiles=scaled_cols // tn, scale=scale),
        grid=(t // tm, ncols // tn),
        in_specs=[pl.BlockSpec((tm, k), lambda i, j: (i, 0)),
                  pl.BlockSpec((None, k, tn), lambda i, j: (layer, 0, col0 // tn + j))],
        out_specs=pl.BlockSpec((tm, tn), lambda i, j: (i, j)),
        out_shape=jax.ShapeDtypeStruct((t, ncols), out_dtype),
        compiler_params=_params(("parallel", "arbitrary")),
        name="in_projection",
    )(a, w_stack)


def _out_proj_kernel(a1_ref, a2_ref, w1_ref, w2_ref, r_ref, o_ref):
    acc = jnp.dot(a1_ref[...], w1_ref[...].astype(BF16), preferred_element_type=F32)
    acc += jnp.dot(a2_ref[...], w2_ref[...].astype(BF16), preferred_element_type=F32)
    o_ref[...] = r_ref[...] + acc


def out_projection(a1, a2, w_stack, layer, resid):
    t, kh = a1.shape
    n = w_stack.shape[-1]
    tm = _pick_tile(t, 1280, ROW_ALIGN)
    tn = _pick_tile(n, 512, LANES)
    return pl.pallas_call(
        _out_proj_kernel,
        grid=(t // tm, n // tn),
        in_specs=[pl.BlockSpec((tm, kh), lambda i, j: (i, 0)),
                  pl.BlockSpec((tm, kh), lambda i, j: (i, 0)),
                  pl.BlockSpec((None, kh, tn), lambda i, j: (layer, 0, j)),
                  pl.BlockSpec((None, kh, tn), lambda i, j: (layer, 1, j)),
                  pl.BlockSpec((tm, tn), lambda i, j: (i, j))],
        out_specs=pl.BlockSpec((tm, tn), lambda i, j: (i, j)),
        out_shape=jax.ShapeDtypeStruct((t, n), F32),
        compiler_params=_params(("parallel", "arbitrary")),
        name="out_projection",
    )(a1, a2, w_stack, w_stack, resid)


def _log2_leave(z2):
    return jnp.maximum(z2, 0.0) + jnp.log2(1.0 + jnp.exp2(-jnp.abs(z2)))


def _attn_chunk(q_ref, k_ref, v_ref, tri_ref, carry_ref, acc_ref, h):
    hs = slice(h * HEAD_DIM, (h + 1) * HEAD_DIM)
    z2 = lax.dot_general(q_ref[:, hs], k_ref[:, hs], (((1,), (1,)), ((), ())), preferred_element_type=F32)
    hi, lo = _split_bf16(_log2_leave(z2))
    sums = jnp.dot(jnp.concatenate([hi, lo], axis=1), tri_ref[...], preferred_element_type=F32)
    carry = carry_ref[h]
    w = jnp.exp2(z2 + sums[:, :Q_TILE] + carry).astype(BF16)
    acc_ref[:, hs] += jnp.dot(w, v_ref[:, hs], preferred_element_type=F32)
    carry_ref[h] = carry + sums[:, Q_TILE:]


def _attn_window(q_ref, k_refs, v_refs, tri_ref, carry_ref, acc_ref, z_ref, hl_ref, i, n_heads):
    shape = (Q_TILE, Q_TILE)
    visible = lax.broadcasted_iota(jnp.int32, shape, 1) < lax.broadcasted_iota(jnp.int32, shape, 0)
    nt = (((1,), (1,)), ((), ()))
    for c in range(ATTN_WINDOW):
        for h in range(n_heads):
            hs = slice(h * HEAD_DIM, (h + 1) * HEAD_DIM)
            z2 = lax.dot_general(q_ref[:, hs], k_refs[c][:, hs], nt, preferred_element_type=F32)
            z_ref[c, h] = z2
            leave = _log2_leave(z2)
            if c == 0:
                leave = jnp.where(visible, leave, 0.0)
            hi, lo = _split_bf16(leave)
            hl_ref[c, h, :, :Q_TILE] = hi
            hl_ref[c, h, :, Q_TILE:] = lo
    for c in range(ATTN_WINDOW):
        tri = tri_ref[...]
        if c > 0:
            present = i >= c
            tri = jnp.where(present, tri, jnp.zeros_like(tri))
        for h in range(n_heads):
            hs = slice(h * HEAD_DIM, (h + 1) * HEAD_DIM)
            sums = jnp.dot(hl_ref[c, h], tri, preferred_element_type=F32)
            expo = z_ref[c, h] + sums[:, :Q_TILE]
            v = v_refs[c][:, hs]
            if c == 0:
                expo = jnp.where(visible, expo, MASKED)
            else:
                expo = jnp.minimum(expo + carry_ref[h], 0.0)
                v = jnp.where(present, v, jnp.zeros_like(v))
            pv = jnp.dot(jnp.exp2(expo).astype(BF16), v, preferred_element_type=F32)
            if c == 0:
                acc_ref[:, hs] = pv
                carry_ref[h] = sums[:, Q_TILE:]
            else:
                acc_ref[:, hs] += pv
                carry_ref[h] += sums[:, Q_TILE:]


def _attn_kernel(q_ref, k0_ref, k1_ref, k2_ref, v0_ref, v1_ref, v2_ref, kv_hbm, tri_ref, g_ref,
                 o_ref, carry_ref, acc_ref, z_ref, hl_ref, kt_ref, vt_ref, flag_ref, sem, *, n_heads, n_qblocks):
    b = pl.program_id(0)
    i = pl.program_id(1)
    width = n_heads * HEAD_DIM
    k_refs = (k0_ref, k1_ref, k2_ref)
    v_refs = (v0_ref, v1_ref, v2_ref)
    kb0 = i - ATTN_WINDOW

    def tail_copies(kb, slot):
        row0 = pl.multiple_of((b * n_qblocks + kb) * Q_TILE, Q_TILE)
        rows = pl.ds(row0, Q_TILE)
        return (pltpu.make_async_copy(kv_hbm.at[rows, pl.ds(width, width)], kt_ref.at[slot], sem.at[0, slot]),
                pltpu.make_async_copy(kv_hbm.at[rows, pl.ds(2 * width, width)], vt_ref.at[slot], sem.at[1, slot]))

    @pl.when(kb0 >= 0)
    def _():
        for cp in tail_copies(kb0, 0):
            cp.start()

    _attn_window(q_ref, k_refs, v_refs, tri_ref, carry_ref, acc_ref, z_ref, hl_ref, i, n_heads)

    def head_flags():
        n = jnp.int32(0)
        for h in range(n_heads):
            f = (jnp.max(carry_ref[h]) > EXP_UNDERFLOW).astype(jnp.int32)
            flag_ref[h] = f
            n = n + f
        return n

    m = carry_ref[0]
    for h in range(1, n_heads):
        m = jnp.maximum(m, carry_ref[h])
    any_left = jnp.logical_and(jnp.max(m) > EXP_UNDERFLOW, kb0 >= 0)

    @pl.when(any_left)
    def _():
        head_flags()

    def tail_cond(state):
        kb, n_active = state
        return jnp.logical_and(kb >= 0, n_active > 0)

    def tail_body(state):
        kb, _ = state
        slot = lax.rem(kb0 - kb, 2)
        for cp in tail_copies(kb, slot):
            cp.wait()

        @pl.when(kb >= 1)
        def _():
            for cp in tail_copies(kb - 1, 1 - slot):
                cp.start()

        for h in range(n_heads):
            @pl.when(flag_ref[h] > 0)
            def _(h=h):
                _attn_chunk(q_ref, kt_ref.at[slot], vt_ref.at[slot], tri_ref, carry_ref, acc_ref, h)
        return kb - 1, head_flags()

    kb_end, _ = lax.while_loop(tail_cond, tail_body, (kb0, any_left.astype(jnp.int32)))

    @pl.when(kb_end >= 0)
    def _():
        for cp in tail_copies(kb_end, lax.rem(kb0 - kb_end, 2)):
            cp.wait()

    o_ref[...] = _rms(acc_ref[...], g_ref[...]).astype(o_ref.dtype)


def attention(qkv, gain, batch, n_heads):
    t = qkv.shape[0]
    width = n_heads * HEAD_DIM
    nq = t // batch // Q_TILE
    j = jnp.arange(2 * Q_TILE)[:, None] % Q_TILE
    s = jnp.arange(2 * Q_TILE)[None, :]
    tri = -jnp.where(s < Q_TILE, j >= s, True).astype(BF16)

    def qmap(b, i):
        return (b * nq + i, 0)

    def kvmap(c, col):
        return lambda b, i: (b * nq + jnp.maximum(i - c, 0), col)

    blk = (Q_TILE, width)
    return pl.pallas_call(
        functools.partial(_attn_kernel, n_heads=n_heads, n_qblocks=nq),
        grid=(batch, nq),
        in_specs=[pl.BlockSpec(blk, qmap)]
        + [pl.BlockSpec(blk, kvmap(c, 1)) for c in range(ATTN_WINDOW)]
        + [pl.BlockSpec(blk, kvmap(c, 2)) for c in range(ATTN_WINDOW)]
        + [pl.BlockSpec(memory_space=pl.ANY),
           pl.BlockSpec((2 * Q_TILE, 2 * Q_TILE), lambda b, i: (0, 0)),
           pl.BlockSpec((1, width), lambda b, i: (0, 0))],
        out_specs=pl.BlockSpec(blk, qmap),
        out_shape=jax.ShapeDtypeStruct((t, width), BF16),
        scratch_shapes=[pltpu.VMEM((n_heads, Q_TILE, Q_TILE), F32),
                        pltpu.VMEM((Q_TILE, width), F32),
                        pltpu.VMEM((ATTN_WINDOW, n_heads, Q_TILE, Q_TILE), F32),
                        pltpu.VMEM((ATTN_WINDOW, n_heads, Q_TILE, 2 * Q_TILE), BF16),
                        pltpu.VMEM((2, Q_TILE, width), BF16),
                        pltpu.VMEM((2, Q_TILE, width), BF16),
                        pltpu.SMEM((n_heads,), jnp.int32),
                        pltpu.SemaphoreType.DMA((2, 2))],
        compiler_params=_params(("parallel", "arbitrary")),
        name="stickbreak_attention",
    )(qkv, qkv, qkv, qkv, qkv, qkv, qkv, qkv, tri, gain.reshape(1, width))


def _ssm_factors(lam_re, lam_im, log_step, b_re, b_im, c_re, c_im):
    depth, g, p, c = b_re.shape
    gp = LANES // c
    jt = g // gp
    lr = lam_re.astype(F32)[:, None]
    li = lam_im.astype(F32)[:, None]
    step = jnp.exp(log_step.astype(F32))[:, None, :, None]
    tau = jnp.arange(SSM_CHUNK + 1, dtype=F32)[None, :, None, None]
    mag = jnp.exp(lr * step * tau)
    ang = li * step * tau
    pw_re = mag * jnp.cos(ang)
    pw_im = mag * jnp.sin(ang)
    abar_re, abar_im = pw_re[:, 1], pw_im[:, 1]
    lr, li = lr[:, 0], li[:, 0]
    den = lr * lr + li * li
    num_re = abar_re - 1.0
    coef_re = ((num_re * lr + abar_im * li) / den)[..., None]
    coef_im = ((abar_im * lr - num_re * li) / den)[..., None]
    br = b_re.astype(F32)
    bi = b_im.astype(F32)
    bbar_re = coef_re * br - coef_im * bi
    bbar_im = coef_re * bi + coef_im * br
    eye = jnp.eye(gp, dtype=F32)

    def expand_b(x):
        x = x.reshape(depth, jt, gp, p, c).transpose(0, 1, 2, 4, 3)
        x = x[:, :, :, :, None, :] * eye[None, None, :, None, :, None]
        return x.reshape(depth, jt, gp * c, gp * p)

    def expand_c(x):
        x = x.astype(F32).reshape(depth, jt, gp, c, p)
        x = x[:, :, :, :, None, :] * eye[None, None, :, None, :, None]
        return x.reshape(depth, jt, gp * c, gp * p)

    def lane_tiles(x):
        return x.reshape(depth, SSM_CHUNK + 1, jt, gp * p).transpose(0, 2, 1, 3)

    return (expand_b(bbar_re), expand_b(bbar_im), expand_c(c_re), expand_c(c_im),
            lane_tiles(pw_re), lane_tiles(pw_im))


def _ssm_build_kernel(bbr_ref, bbi_ref, ctr_ref, cti_ref, pwr_ref, pwi_ref, m_ref, win_ref, wout_ref):
    tc = SSM_CHUNK
    bb_re = bbr_ref[...]
    bb_im = bbi_ref[...]
    ct_re = ctr_ref[...]
    ct_im = cti_ref[...]
    half = bb_re.shape[1]
    nt = (((1,), (1,)), ((), ()))

    def dot_nt(a, b):
        ah, al = _split_bf16(a)
        bh, bl = _split_bf16(b)
        return (lax.dot_general(ah, bh, nt, preferred_element_type=F32)
                + lax.dot_general(al, bh, nt, preferred_element_type=F32)
                + lax.dot_general(ah, bl, nt, preferred_element_type=F32))

    def rows(n):
        return slice(n * LANES, (n + 1) * LANES)

    zero = jnp.zeros((LANES, LANES), BF16)
    for s in range(tc):
        for r in range(s):
            m_ref[rows(s), rows(r)] = zero
    for tau in range(tc + 1):
        pr = pwr_ref[tau:tau + 1, :]
        pi = pwi_ref[tau:tau + 1, :]
        cp_re = ct_re * pr - ct_im * pi
        cp_im = ct_re * pi + ct_im * pr
        if tau < tc:
            lag_block = (dot_nt(bb_re, cp_re) - dot_nt(bb_im, cp_im)).astype(BF16)
            for s in range(tc - tau):
                m_ref[rows(s), rows(s + tau)] = lag_block
            s = tc - 1 - tau
            win_ref[rows(s), :half] = (bb_re * pr - bb_im * pi).astype(BF16)
            win_ref[rows(s), half:] = (bb_re * pi + bb_im * pr).astype(BF16)
        if tau >= 1:
            wout_ref[rows(tau - 1), :half] = cp_re.astype(BF16)
            wout_ref[rows(tau - 1), half:] = (-cp_im).astype(BF16)


def ssm_operators(factors):
    bb_re, bb_im, ct_re, ct_im, pw_re, pw_im = factors
    depth, jt, cw, sw = bb_re.shape
    kdim = SSM_CHUNK * LANES
    fac = pl.BlockSpec((None, None, cw, sw), lambda d, j: (d, j, 0, 0))
    pws = pl.BlockSpec((None, None, SSM_CHUNK + 1, sw), lambda d, j: (d, j, 0, 0))
    out = pl.BlockSpec((None, None, kdim, kdim), lambda d, j: (d, j, 0, 0))
    assert 2 * sw == kdim
    shape = jax.ShapeDtypeStruct((depth, jt, kdim, kdim), BF16)
    return pl.pallas_call(
        _ssm_build_kernel,
        grid=(depth, jt),
        in_specs=[fac, fac, fac, fac, pws, pws],
        out_specs=[out, out, out],
        out_shape=[shape, shape, shape],
        compiler_params=_params(("parallel", "parallel")),
        name="ssm_operators",
    )(bb_re, bb_im, ct_re, ct_im, pw_re, pw_im)


def _gelu_tanh(x):
    return 0.5 * x * (1.0 + jnp.tanh(math.sqrt(2.0 / math.pi) * (x + 0.044715 * (x * x * x))))


def _ssm_kernel(u_ref, m_ref, win_ref, wout_ref, pwr_ref, pwi_ref, d_ref, o_ref, e_ref, sprev_ref, *, n_chunks):
    tc = SSM_CHUNK
    half = pwr_ref.shape[-1]
    uv = jnp.concatenate([u_ref[pl.ds(r, n_chunks, stride=tc), :] for r in range(tc)], axis=1)
    uvb = uv.astype(BF16)
    e_ref[...] = jnp.dot(uvb, win_ref[...], preferred_element_type=F32)
    a_re = pwr_ref[tc:tc + 1, :]
    a_im = pwi_ref[tc:tc + 1, :]

    def step(n, state):
        s_re, s_im = state
        sprev_ref[pl.ds(n, 1), :half] = s_re
        sprev_ref[pl.ds(n, 1), half:] = s_im
        e = e_ref[pl.ds(n, 1), :]
        return (a_re * s_re - a_im * s_im + e[:, :half],
                a_re * s_im + a_im * s_re + e[:, half:])

    zero = jnp.zeros((1, half), F32)
    lax.fori_loop(0, n_chunks, step, (zero, zero))
    y = jnp.dot(uvb, m_ref[...], preferred_element_type=F32)
    y += lax.dot_general(sprev_ref[...].astype(BF16), wout_ref[...], (((1,), (1,)), ((), ())),
                         preferred_element_type=F32)
    d = d_ref[...]
    for r in range(tc):
        cols = slice(r * LANES, (r + 1) * LANES)
        o_ref[pl.ds(r, n_chunks, stride=tc), :] = _gelu_tanh(y[:, cols] + d * uv[:, cols])


def ssm_scan(u, ops, pw, layer, d_skip, batch):
    m_intra, w_in, w_out = ops
    pw_re, pw_im = pw
    t, width = u.shape
    lp = t // batch
    jt = width // LANES
    nc = lp // SSM_CHUNK
    kdim = SSM_CHUNK * LANES
    sw = pw_re.shape[-1]
    op = pl.BlockSpec((None, None, kdim, kdim), lambda j, b: (layer, j, 0, 0))
    pws = pl.BlockSpec((None, None, SSM_CHUNK + 1, sw), lambda j, b: (layer, j, 0, 0))
    return pl.pallas_call(
        functools.partial(_ssm_kernel, n_chunks=nc),
        grid=(jt, batch),
        in_specs=[pl.BlockSpec((lp, LANES), lambda j, b: (b, j)), op, op, op, pws, pws,
                  pl.BlockSpec((1, LANES), lambda j, b: (0, j))],
        out_specs=pl.BlockSpec((lp, LANES), lambda j, b: (b, j)),
        out_shape=jax.ShapeDtypeStruct((t, width), F32),
        scratch_shapes=[pltpu.VMEM((nc, 2 * sw), F32), pltpu.VMEM((nc, 2 * sw), F32)],
        compiler_params=_params(("parallel", "arbitrary")),
        name="ssm_scan",
    )(u, m_intra, w_in, w_out, pw_re, pw_im, d_skip.reshape(1, width))


def _glu_kernel(g_ref, w_ref, b_ref, gain_ref, o_ref):
    g = g_ref[...]
    zz = jnp.dot(g.astype(BF16), w_ref[...], preferred_element_type=F32) + b_ref[...]
    y = g * (1.0 / (1.0 + jnp.exp(-zz)))
    o_ref[...] = _rms(y, gain_ref[...]).astype(o_ref.dtype)


def glu_norm(g, w_stack, layer, bias, gain):
    t, width = g.shape
    tm = _pick_tile(t, 640, ROW_ALIGN)
    return pl.pallas_call(
        _glu_kernel,
        grid=(t // tm,),
        in_specs=[pl.BlockSpec((tm, width), lambda i: (i, 0)),
                  pl.BlockSpec((None, width, width), lambda i: (layer, 0, 0)),
                  pl.BlockSpec((1, width), lambda i: (0, 0)),
                  pl.BlockSpec((1, width), lambda i: (0, 0))],
        out_specs=pl.BlockSpec((tm, width), lambda i: (i, 0)),
        out_shape=jax.ShapeDtypeStruct((t, width), BF16),
        compiler_params=_params(("parallel",)),
        name="glu_norm",
    )(g, w_stack, bias.reshape(1, width), gain.reshape(1, width))


def _slab_pitch(d):
    return d // LANES + 1


def _slab_store(ref, first, x):
    n, d = x.shape
    pitch = _slab_pitch(d)
    for c in range(d // LANES):
        ref[pl.ds(first * pitch + c, n, stride=pitch), :] = x[:, c * LANES:(c + 1) * LANES]
    ref[pl.ds(first * pitch + d // LANES, n, stride=pitch), :] = jnp.zeros((n, LANES), x.dtype)


def _slab_tile(ref, first, n, c, pitch):
    return ref[pl.ds(first * pitch + c, n, stride=pitch), :]


def _route_kernel(h_ref, gain_ref, wh_ref, wl_ref, bias_ref, hn_ref, eid_ref, gate_ref, *, n_groups, per_group):
    hn = _rms(h_ref[...], gain_ref[...])
    _slab_store(hn_ref, 0, hn)
    xh, xl = _split_bf16(hn)
    wh = wh_ref[...]
    logits = (jnp.dot(xh, wh, preferred_element_type=F32) + jnp.dot(xl, wh, preferred_element_type=F32)
              + jnp.dot(xh, wl_ref[...], preferred_element_type=F32) + bias_ref[...])
    lane = lax.broadcasted_iota(jnp.int32, logits.shape, 1)

    def first_lane_of_max(vals, top):
        return jnp.min(jnp.where(vals == top, lane, LANES), axis=-1, keepdims=True)

    gmask = lane < n_groups
    lg = jnp.where(gmask, logits, MASKED)
    g_top = jnp.max(lg, axis=-1, keepdims=True)
    g_prob = 1.0 / jnp.sum(jnp.where(gmask, jnp.exp(lg - g_top), 0.0), axis=-1, keepdims=True)
    g_idx = first_lane_of_max(lg, g_top)
    lo_lane = n_groups + g_idx * per_group
    emask = jnp.logical_and(lane >= lo_lane, lane < lo_lane + per_group)
    le = jnp.where(emask, logits, MASKED)
    m1 = jnp.max(le, axis=-1, keepdims=True)
    i1 = first_lane_of_max(le, m1)
    denom = jnp.sum(jnp.where(emask, jnp.exp(le - m1), 0.0), axis=-1, keepdims=True)
    le2 = jnp.where(lane == i1, MASKED, le)
    m2 = jnp.max(le2, axis=-1, keepdims=True)
    i2 = first_lane_of_max(le2, m2)
    p1 = 1.0 / denom
    p2 = jnp.exp(m2 - m1) / denom
    psum = p1 + p2
    eid_ref[...] = jnp.where(lane == 0, i1 - n_groups, jnp.where(lane == 1, i2 - n_groups, 0))
    gate_ref[...] = jnp.where(lane == 0, g_prob * (p1 / psum), jnp.where(lane == 1, g_prob * (p2 / psum), 0.0))


def norm_route(h, gain, w_router, b_router, n_groups, per_group):
    t, d = h.shape
    tm = _pick_tile(t, 256, ROW_ALIGN)
    pitch = _slab_pitch(d)
    n_out = w_router.shape[-1]
    w_pad = jnp.zeros((d, LANES), F32).at[:, :n_out].set(w_router)
    wh, wl = _split_bf16(w_pad)
    bias = jnp.zeros((1, LANES), F32).at[0, :n_out].set(b_router)
    return pl.pallas_call(
        functools.partial(_route_kernel, n_groups=n_groups, per_group=per_group),
        grid=(t // tm,),
        in_specs=[pl.BlockSpec((tm, d), lambda i: (i, 0)),
                  pl.BlockSpec((1, d), lambda i: (0, 0)),
                  pl.BlockSpec((d, LANES), lambda i: (0, 0)),
                  pl.BlockSpec((d, LANES), lambda i: (0, 0)),
                  pl.BlockSpec((1, LANES), lambda i: (0, 0))],
        out_specs=[pl.BlockSpec((tm * pitch, LANES), lambda i: (i, 0)),
                   pl.BlockSpec((tm, LANES), lambda i: (i, 0)),
                   pl.BlockSpec((tm, LANES), lambda i: (i, 0))],
        out_shape=[jax.ShapeDtypeStruct((t * pitch, LANES), F32),
                   jax.ShapeDtypeStruct((t, LANES), jnp.int32),
                   jax.ShapeDtypeStruct((t, LANES), F32)],
        compiler_params=_params(("parallel",)),
        name="norm_route",
    )(h, gain.reshape(1, d), wh, wl, bias)


def dispatch_plan(eid, n_experts):
    n_tok, top_k = eid.shape
    n_assign = n_tok * top_k
    n_blocks = -(-n_assign // EXPERT_ROWS) + n_experts
    expert = eid.reshape(-1)
    iota = jnp.arange(n_assign, dtype=jnp.int32)
    experts = jnp.arange(n_experts, dtype=jnp.int32)
    expert_sorted, order = lax.sort((expert, iota), num_keys=1, is_stable=True)
    counts = jnp.sum((expert[:, None] == experts[None, :]).astype(jnp.int32), axis=0)
    padded = (counts + EXPERT_ROWS - 1) // EXPERT_ROWS * EXPERT_ROWS
    pad_end = jnp.cumsum(padded)
    pad_start = pad_end - padded
    start = jnp.cumsum(counts) - counts
    shift = pad_start - start
    dshift = shift - jnp.concatenate([jnp.zeros((1,), jnp.int32), shift[:-1]])
    dest = iota + jnp.sum(jnp.where(expert_sorted[:, None] >= experts[None, :], dshift[None, :], 0), axis=1)
    _, pos = lax.sort((order, dest), num_keys=1)
    blk_row0 = jnp.arange(n_blocks, dtype=jnp.int32) * EXPERT_ROWS
    block_expert = jnp.minimum(
        jnp.sum((pad_end[None, :] <= blk_row0[:, None]).astype(jnp.int32), axis=1), n_experts - 1)
    block_shift = jnp.sum(jnp.where(block_expert[:, None] == experts[None, :], shift[None, :], 0), axis=1)
    block_src = blk_row0 - block_shift
    n_used = (pad_end[-1] // EXPERT_ROWS).astype(jnp.int32).reshape(1)
    pos = pos.reshape(n_tok // COMBINE_ROWS, COMBINE_ROWS, top_k).transpose(0, 2, 1).reshape(-1)
    return order // top_k, block_expert, block_src, n_used, pos


def _gather_slabs(src_hbm, dst_ref, sem, idx_ref, base, n_slabs, d, limit=None, inline=False):
    pitch = _slab_pitch(d)
    tiles = d // LANES

    def start(r):
        at = base + r
        if limit is not None:
            at = jnp.minimum(at, limit)
        src = src_hbm.at[pl.ds(idx_ref[at] * pitch, tiles), :]
        pltpu.make_async_copy(src, dst_ref.at[pl.ds(r * pitch, tiles), :], sem).start()

    if inline:
        for r in range(n_slabs):
            start(r)
    else:
        def body(r, c):
            start(r)
            return c
        lax.fori_loop(0, n_slabs, body, 0, unroll=GATHER_UNROLL)


def _wait_slabs(src_hbm, dst_ref, sem, n_slabs, d):
    rows = n_slabs * (d // LANES)
    pltpu.make_async_copy(src_hbm.at[pl.ds(0, rows), :], dst_ref.at[pl.ds(0, rows), :], sem).wait()


def _moe_kernel(be_ref, bs_ref, nu_ref, tok_ref, x_hbm, wg_ref, wu_ref, wd_ref, o_ref, xbuf0, xbuf1, sem):
    i = pl.program_id(0)
    n_used = nu_ref[0]
    last = tok_ref.shape[0] - 1
    bufs = (xbuf0, xbuf1)
    d = wg_ref.shape[0]
    pitch = _slab_pitch(d)

    def fetch(blk, s, inline):
        _gather_slabs(x_hbm, bufs[s], sem.at[s], tok_ref, bs_ref[blk], EXPERT_ROWS, d, limit=last, inline=inline)

    @pl.when(jnp.logical_and(i == 0, n_used > 0))
    def _():
        fetch(0, 0, False)

    def step(cur, nxt):
        _wait_slabs(x_hbm, bufs[cur], sem.at[cur], EXPERT_ROWS, d)
        fetch(jnp.minimum(i + 1, n_used - 1), nxt, True)
        x = jnp.concatenate([_slab_tile(bufs[cur], 0, EXPERT_ROWS, c, pitch).astype(BF16)
                             for c in range(d // LANES)], axis=1)
        hg = jnp.dot(x, wg_ref[...], preferred_element_type=F32)
        hu = jnp.dot(x, wu_ref[...], preferred_element_type=F32)
        hid = hg * (1.0 / (1.0 + jnp.exp(-hg))) * hu
        _slab_store(o_ref, 0, jnp.dot(hid.astype(BF16), wd_ref[...], preferred_element_type=F32))

        @pl.when(i + 1 >= n_used)
        def _():
            _wait_slabs(x_hbm, bufs[nxt], sem.at[nxt], EXPERT_ROWS, d)

    for parity in range(2):
        @pl.when(jnp.logical_and(i < n_used, lax.rem(i, 2) == parity))
        def _(parity=parity):
            step(parity, 1 - parity)

    @pl.when(i >= n_used)
    def _():
        o_ref[...] = jnp.zeros_like(o_ref)


def expert_blocks(hn, plan, w_gate, w_up, w_down, layer):
    tok_sorted, block_expert, block_src, n_used, _ = plan
    d, f = w_gate.shape[-2:]
    n_blocks = block_expert.shape[0]
    n_rows = n_blocks * EXPERT_ROWS
    buf_rows = EXPERT_ROWS * _slab_pitch(d)

    def wmap(i, be, bs, nu, tok):
        return (layer, be[i], 0, 0)

    return pl.pallas_call(
        _moe_kernel,
        grid_spec=pltpu.PrefetchScalarGridSpec(
            num_scalar_prefetch=4,
            grid=(n_blocks,),
            in_specs=[pl.BlockSpec(memory_space=pl.ANY),
                      pl.BlockSpec((None, None, d, f), wmap),
                      pl.BlockSpec((None, None, d, f), wmap),
                      pl.BlockSpec((None, None, f, d), wmap)],
            out_specs=pl.BlockSpec((buf_rows, LANES), lambda i, be, bs, nu, tok: (i, 0)),
            scratch_shapes=[pltpu.VMEM((buf_rows, LANES), F32), pltpu.VMEM((buf_rows, LANES), F32),
                            pltpu.SemaphoreType.DMA((2,))]),
        out_shape=jax.ShapeDtypeStruct((n_blocks * buf_rows, LANES), F32),
        compiler_params=_params(("arbitrary",)),
        name="expert_blocks",
    )(block_expert, block_src, n_used, tok_sorted, hn, w_gate, w_up, w_down)


def _combine_kernel(pos_ref, y_hbm, h_ref, gate_ref, gain_ref, oh_ref, on_ref, ybuf0, ybuf1, sem, *, top_k):
    i = pl.program_id(0)
    n = pl.num_programs(0)
    rows = COMBINE_ROWS * top_k
    bufs = (ybuf0, ybuf1)
    d = h_ref.shape[-1]
    pitch = _slab_pitch(d)

    def fetch(blk, s, inline):
        _gather_slabs(y_hbm, bufs[s], sem.at[s], pos_ref, blk * rows, rows, d, inline=inline)

    @pl.when(i == 0)
    def _():
        fetch(0, 0, False)

    def step(cur, nxt):
        _wait_slabs(y_hbm, bufs[cur], sem.at[cur], rows, d)
        fetch(jnp.minimum(i + 1, n - 1), nxt, True)
        gates = [gate_ref[:, k:k + 1] for k in range(top_k)]
        tiles = []
        for c in range(d // LANES):
            tile = h_ref[:, c * LANES:(c + 1) * LANES]
            for k in range(top_k):
                tile = tile + gates[k] * _slab_tile(bufs[cur], k * COMBINE_ROWS, COMBINE_ROWS, c, pitch)
            tiles.append(tile)
        h_new = jnp.concatenate(tiles, axis=1)
        oh_ref[...] = h_new
        on_ref[...] = _rms(h_new, gain_ref[...]).astype(on_ref.dtype)

        @pl.when(i == n - 1)
        def _():
            _wait_slabs(y_hbm, bufs[nxt], sem.at[nxt], rows, d)

    for parity in range(2):
        @pl.when(lax.rem(i, 2) == parity)
        def _(parity=parity):
            step(parity, 1 - parity)


def combine(h, y_rows, pos, gate, gain, norm_dtype, top_k):
    t, d = h.shape
    buf_rows = COMBINE_ROWS * top_k * _slab_pitch(d)
    return pl.pallas_call(
        functools.partial(_combine_kernel, top_k=top_k),
        grid_spec=pltpu.PrefetchScalarGridSpec(
            num_scalar_prefetch=1,
            grid=(t // COMBINE_ROWS,),
            in_specs=[pl.BlockSpec(memory_space=pl.ANY),
                      pl.BlockSpec((COMBINE_ROWS, d), lambda i, pos: (i, 0)),
                      pl.BlockSpec((COMBINE_ROWS, LANES), lambda i, pos: (i, 0)),
                      pl.BlockSpec((1, d), lambda i, pos: (0, 0))],
            out_specs=[pl.BlockSpec((COMBINE_ROWS, d), lambda i, pos: (i, 0)),
                       pl.BlockSpec((COMBINE_ROWS, d), lambda i, pos: (i, 0))],
            scratch_shapes=[pltpu.VMEM((buf_rows, LANES), F32), pltpu.VMEM((buf_rows, LANES), F32),
                            pltpu.SemaphoreType.DMA((2,))]),
        out_shape=[jax.ShapeDtypeStruct((t, d), F32), jax.ShapeDtypeStruct((t, d), norm_dtype)],
        compiler_params=_params(("arbitrary",)),
        name="combine",
    )(pos, y_rows, h, gate, gain.reshape(1, d))


def kernel(x, meta_tokens, ln_mix, w_in, ssm_lam_re, ssm_lam_im, ssm_log_step, ssm_b_re, ssm_b_im, ssm_c_re, ssm_c_im, ssm_d, w_glu, b_glu, g_attn_out, g_ssm_out, w_out, ln_ffn, w_route_group, b_route_group, w_route_expert, b_route_expert, w_gate, w_up, w_down, ln_final):
    batch, seq, d_model = x.shape
    depth = w_in.shape[0]
    n_meta = meta_tokens.shape[0]
    ssm_width = ssm_d.shape[-1]
    attn_width = g_attn_out.shape[-1]
    n_heads = attn_width // HEAD_DIM
    n_groups = w_route_group.shape[-1]
    n_experts = w_route_expert.shape[-1]
    top_k = 2
    length = n_meta + seq
    lp = -(-length // ROW_ALIGN) * ROW_ALIGN
    t = batch * lp
    assert attn_width == ssm_width

    meta = jnp.broadcast_to(meta_tokens.astype(x.dtype)[None], (batch, n_meta, d_model))
    h = jnp.concatenate([meta, x, jnp.zeros((batch, lp - length, d_model), x.dtype)], axis=1).reshape(t, d_model)

    w_glu_b = w_glu.astype(BF16)
    w_gate_b = w_gate.astype(BF16)
    w_up_b = w_up.astype(BF16)
    w_down_b = w_down.astype(BF16)
    factors = _ssm_factors(ssm_lam_re, ssm_lam_im, ssm_log_step, ssm_b_re, ssm_b_im, ssm_c_re, ssm_c_im)
    ssm_ops = ssm_operators(factors)

    hn = rmsnorm(h, ln_mix[0], BF16)
    for l in range(depth):
        qkv = in_projection(hn, w_in, l, 0, 3 * attn_width, BF16, scaled_cols=attn_width,
                            scale=math.log2(math.e) * HEAD_DIM ** -0.5)
        u = in_projection(hn, w_in, l, 3 * attn_width, ssm_width, F32)
        attn_n = attention(qkv, g_attn_out[l], batch, n_heads)
        g = ssm_scan(u, ssm_ops, factors[4:], l, ssm_d[l], batch)
        ssm_n = glu_norm(g, w_glu_b, l, b_glu[l], g_ssm_out[l])
        h = out_projection(attn_n, ssm_n, w_out, l, h)
        w_router = jnp.concatenate([w_route_group[l], w_route_expert[l]], axis=-1)
        b_router = jnp.concatenate([b_route_group[l], b_route_expert[l]], axis=-1)
        hn2, eid, gate = norm_route(h, ln_ffn[l], w_router, b_router, n_groups, n_experts // n_groups)
        plan = dispatch_plan(eid[:, :top_k], n_experts)
        y_rows = expert_blocks(hn2, plan, w_gate_b, w_up_b, w_down_b, l)
        last = l == depth - 1
        next_gain = ln_final if last else ln_mix[l + 1]
        h, hn = combine(h, y_rows, plan[4], gate, next_gain, F32 if last else BF16, top_k)
    return hn.reshape(batch, lp, d_model)[:, n_meta:length]
```
